```python
import math
import jax, jax.numpy as jnp
from jax import lax
import numpy as np

D_MODEL = 1024
BATCH = 2
SEQ = 8192
DEPTH = 1

MEM_LEN = 256
EPS = 1e-6
CHUNK = 64

SSD_D_INNER = D_MODEL
SSD_HEAD_DIM = 64
SSD_HEADS = SSD_D_INNER // SSD_HEAD_DIM
SSD_GROUPS = 2
SSD_STATE = 128
SSD_CONV = 4
SSD_CONV_CH = SSD_D_INNER + 2 * SSD_GROUPS * SSD_STATE
DT_MIN = 1e-3
DT_MAX = 1e-1

GLA_HEADS = 4
GLA_DK = D_MODEL // 2
GLA_DV = D_MODEL
GLA_HEAD_K = GLA_DK // GLA_HEADS
GLA_HEAD_V = GLA_DV // GLA_HEADS
GLA_GATE_RANK = 16
GLA_TAU = 16.0

XATTN_HEADS = 4
XATTN_HEAD_DIM = D_MODEL // XATTN_HEADS

D_FF = -((-8 * D_MODEL) // (3 * 256)) * 256

IN_SIZES = (SSD_D_INNER, SSD_CONV_CH, SSD_HEADS,
            GLA_DK, GLA_DK, GLA_DV, GLA_DV, GLA_GATE_RANK,
            D_MODEL, D_MODEL)
IN_WIDTH = sum(IN_SIZES)

kernel_name = "hybrid_ssd_gla_gated_merge_block"


def rmsnorm(x, g):
    xf = x.astype(jnp.float32)
    y = xf * lax.rsqrt(jnp.mean(xf * xf, axis=-1, keepdims=True) + EPS)
    return (y * g.astype(jnp.float32)).astype(x.dtype)


def causal_dwconv(x, w, b):
    y = lax.conv_general_dilated(x, w, window_strides=(1,), padding=[(SSD_CONV - 1, 0)],
                                 dimension_numbers=('NWC', 'WIO', 'NWC'),
                                 feature_group_count=x.shape[-1])
    return y + b


def segsum_exp(a):
    L = a.shape[-1]
    cs = jnp.cumsum(a, axis=-1)
    mask = jnp.tril(jnp.ones((L, L), dtype=bool))
    diff = cs[..., :, None] - cs[..., None, :]
    return jnp.where(mask, jnp.exp(jnp.where(mask, diff, 0.0)), 0.0)


def ssd_mixer(z, xBC, dt_raw, conv_w, conv_b, dt_bias, A_log, D_skip, norm_g):
    Bsz, S, _ = z.shape
    nc = S // CHUNK
    G, R, P, N = SSD_GROUPS, SSD_HEADS // SSD_GROUPS, SSD_HEAD_DIM, SSD_STATE
    xBC = jax.nn.silu(causal_dwconv(xBC, conv_w, conv_b)).astype(jnp.float32)
    xs, Bm, Cm = jnp.split(xBC, [SSD_D_INNER, SSD_D_INNER + G * N], axis=-1)
    dt = jax.nn.softplus(dt_raw.astype(jnp.float32) + dt_bias.astype(jnp.float32))
    A = -jnp.exp(A_log.astype(jnp.float32))
    x_heads = xs.reshape(Bsz, S, SSD_HEADS, P)
    X = (x_heads * dt[..., None]).reshape(Bsz, nc, CHUNK, G, R, P)
    a = (dt * A).reshape(Bsz, nc, CHUNK, G, R).transpose(0, 3, 4, 1, 2)
    Bc = Bm.reshape(Bsz, nc, CHUNK, G, N)
    Cc = Cm.reshape(Bsz, nc, CHUNK, G, N)
    a_cs = jnp.cumsum(a, axis=-1)
    Lmat = segsum_exp(a)
    CB = jnp.einsum('bclgn,bcsgn->bcgls', Cc, Bc)
    y_diag = jnp.einsum('bcgls,bgrcls,bcsgrp->bclgrp', CB, Lmat, X)
    decay_states = jnp.exp(a_cs[..., -1:] - a_cs)
    states = jnp.einsum('bclgn,bgrcl,bclgrp->bcgrpn', Bc, decay_states, X)
    chunk_decay = jnp.exp(a_cs[..., -1])

    def chunk_step(state, inp):
        st_c, dec_c = inp
        return state * dec_c[..., None, None] + st_c, state

    init = jnp.zeros((Bsz, G, R, P, N), jnp.float32)
    _, prev = lax.scan(chunk_step, init, (jnp.moveaxis(states, 1, 0), jnp.moveaxis(chunk_decay, 3, 0)))
    prev = jnp.moveaxis(prev, 0, 1)
    y_off = jnp.einsum('bclgn,bcgrpn,bgrcl->bclgrp', Cc, prev, jnp.exp(a_cs))
    y = (y_diag + y_off).reshape(Bsz, S, SSD_HEADS, P) + D_skip.astype(jnp.float32)[:, None] * x_heads
    y = y.reshape(Bsz, S, SSD_D_INNER) * jax.nn.silu(z.astype(jnp.float32))
    y = rmsnorm(y.reshape(Bsz, S, G, SSD_D_INNER // G), norm_g.reshape(G, SSD_D_INNER // G))
    return y.reshape(Bsz, S, SSD_D_INNER).astype(z.dtype)


def gla_mixer(q, k, v, r, a1, w_a2, b_a, norm_g):
    Bsz, S, _ = q.shape
    nc = S // CHUNK
    H, dk, dv = GLA_HEADS, GLA_HEAD_K, GLA_HEAD_V
    log_alpha = jax.nn.log_sigmoid((a1 @ w_a2 + b_a).astype(jnp.float32)) / GLA_TAU
    qc = q.astype(jnp.float32).reshape(Bsz, nc, CHUNK, H, dk) * (dk ** -0.5)
    kc = k.astype(jnp.float32).reshape(Bsz, nc, CHUNK, H, dk)
    vc = v.astype(jnp.float32).reshape(Bsz, nc, CHUNK, H, dv)
    bcum = jnp.cumsum(log_alpha.reshape(Bsz, nc, CHUNK, H, dk), axis=2)
    b_last = bcum[:, :, -1:]
    q_t = qc * jnp.exp(bcum)
    k_t = kc * jnp.exp(-bcum)
    k_h = kc * jnp.exp(b_last - bcum)
    mask = jnp.tril(jnp.ones((CHUNK, CHUNK), dtype=bool))
    att = jnp.where(mask, jnp.einsum('bclhd,bcshd->bchls', q_t, k_t), 0.0)
    o_intra = jnp.einsum('bchls,bcshv->bclhv', att, vc)
    U = jnp.einsum('bclhd,bclhv->bchdv', k_h, vc)
    dec = jnp.exp(b_last[:, :, 0])

    def chunk_step(state, inp):
        u_c, d_c = inp
        return state * d_c[..., None] + u_c, state

    init = jnp.zeros((Bsz, H, dk, dv), jnp.float32)
    _, prev = lax.scan(chunk_step, init, (jnp.moveaxis(U, 1, 0), jnp.moveaxis(dec, 1, 0)))
    prev = jnp.moveaxis(prev, 0, 1)
    o_inter = jnp.einsum('bclhd,bchdv->bclhv', q_t, prev)
    o = (o_intra + o_inter).reshape(Bsz, S, H, dv)
    o = rmsnorm(o, norm_g).reshape(Bsz, S, GLA_DV)
    return (o * jax.nn.silu(r.astype(jnp.float32))).astype(q.dtype)


def hybrid_mixer(n, w_in, conv_w, conv_b, dt_bias, A_log, D_skip, ssd_norm,
                 gla_w_a2, gla_b_a, gla_norm, w_up_ssd, w_up_gla, w_o):
    proj = n @ w_in
    idx = np.cumsum(IN_SIZES)[:-1].tolist()
    z, xBC, dt_raw, q, k, v, r, a1, g_ssd, g_gla = jnp.split(proj, idx, axis=-1)
    y_ssd = ssd_mixer(z, xBC, dt_raw, conv_w, conv_b, dt_bias, A_log, D_skip, ssd_norm)
    y_gla = gla_mixer(q, k, v, r, a1, gla_w_a2, gla_b_a, gla_norm)
    merged = jax.nn.sigmoid(g_ssd) * (y_ssd @ w_up_ssd) + jax.nn.sigmoid(g_gla) * (y_gla @ w_up_gla)
    return merged @ w_o


def mem_cross_attention(n, m, w_xq, w_xkv, w_xo):
    Bsz, S, _ = n.shape
    q = (n @ w_xq).reshape(Bsz, S, XATTN_HEADS, XATTN_HEAD_DIM)
    k, v = jnp.split(m @ w_xkv, 2, axis=-1)
    k = k.reshape(Bsz, -1, XATTN_HEADS, XATTN_HEAD_DIM)
    v = v.reshape(Bsz, -1, XATTN_HEADS, XATTN_HEAD_DIM)
    s = jnp.einsum('bshd,bmhd->bhsm', q, k).astype(jnp.float32) * (XATTN_HEAD_DIM ** -0.5)
    p = jax.nn.softmax(s, axis=-1).astype(v.dtype)
    o = jnp.einsum('bhsm,bmhd->bshd', p, v).reshape(Bsz, S, D_MODEL)
    return o @ w_xo


def swiglu(n, w_ffn_in, w_ffn_out):
    g, u = jnp.split(n @ w_ffn_in, 2, axis=-1)
    return (jax.nn.silu(g) * u) @ w_ffn_out


def setup_inputs(seed: int = 0) -> dict:
    key = jax.random.key(seed)
    ks = jax.random.split(key, 26)
    f32 = jnp.float32
    L = DEPTH

    def nrm(k, shape, scale):
        return jax.random.normal(k, shape, f32) * scale

    def gain(k, shape):
        return 1.0 + 0.02 * jax.random.normal(k, shape, f32)

    dt = jnp.exp(jax.random.uniform(ks[6], (L, SSD_HEADS), f32, math.log(DT_MIN), math.log(DT_MAX)))
    dt_bias = dt + jnp.log(-jnp.expm1(-dt))
    A_log = jnp.log(jax.random.uniform(ks[7], (L, SSD_HEADS), f32, 1.0, 16.0))
    return {
        "x": jax.random.normal(ks[0], (BATCH, SEQ, D_MODEL), f32),
        "mem": jax.random.normal(ks[1], (BATCH, MEM_LEN, D_MODEL), f32),
        "norm_mix": gain(ks[2], (L, D_MODEL)),
        "w_in": nrm(ks[3], (L, D_MODEL, IN_WIDTH), D_MODEL ** -0.5),
        "ssd_conv_w": nrm(ks[4], (L, SSD_CONV, 1, SSD_CONV_CH), SSD_CONV ** -0.5),
        "ssd_conv_b": nrm(ks[5], (L, SSD_CONV_CH), 0.02),
        "ssd_dt_bias": dt_bias,
        "ssd_A_log": A_log,
        "ssd_D": 1.0 + 0.1 * jax.random.normal(ks[8], (L, SSD_HEADS), f32),
        "ssd_norm": gain(ks[9], (L, SSD_D_INNER)),
        "gla_w_a2": nrm(ks[10], (L, GLA_GATE_RANK, GLA_DK), GLA_GATE_RANK ** -0.5),
        "gla_b_a": nrm(ks[11], (L, GLA_DK), 0.02),
        "gla_norm": gain(ks[12], (L, GLA_HEAD_V)),
        "w_up_ssd": nrm(ks[13], (L, SSD_D_INNER, D_MODEL), SSD_D_INNER ** -0.5),
        "w_up_gla": nrm(ks[14], (L, GLA_DV, D_MODEL), GLA_DV ** -0.5),
        "w_o": nrm(ks[15], (L, D_MODEL, D_MODEL), D_MODEL ** -0.5),
        "norm_xattn": gain(ks[16], (L, D_MODEL)),
        "norm_mem": gain(ks[17], (L, D_MODEL)),
        "w_xq": nrm(ks[18], (L, D_MODEL, D_MODEL), D_MODEL ** -0.5),
        "w_xkv": nrm(ks[19], (L, D_MODEL, 2 * D_MODEL), D_MODEL ** -0.5),
        "w_xo": nrm(ks[20], (L, D_MODEL, D_MODEL), D_MODEL ** -0.5),
        "norm_ffn": gain(ks[21], (L, D_MODEL)),
        "w_ffn_in": nrm(ks[22], (L, D_MODEL, 2 * D_FF), D_MODEL ** -0.5),
        "w_ffn_out": nrm(ks[23], (L, D_FF, D_MODEL), D_FF ** -0.5),
        "norm_final": gain(ks[24], (D_MODEL,)),
    }


def reference(x, mem, norm_mix, w_in, ssd_conv_w, ssd_conv_b, ssd_dt_bias, ssd_A_log, ssd_D, ssd_norm,
              gla_w_a2, gla_b_a, gla_norm, w_up_ssd, w_up_gla, w_o, norm_xattn, norm_mem, w_xq, w_xkv,
              w_xo, norm_ffn, w_ffn_in, w_ffn_out, norm_final):
    h = x
    for i in range(DEPTH):
        h = h + hybrid_mixer(rmsnorm(h, norm_mix[i]), w_in[i], ssd_conv_w[i], ssd_conv_b[i], ssd_dt_bias[i],
                             ssd_A_log[i], ssd_D[i], ssd_norm[i], gla_w_a2[i], gla_b_a[i], gla_norm[i],
                             w_up_ssd[i], w_up_gla[i], w_o[i])
        h = h + mem_cross_attention(rmsnorm(h, norm_xattn[i]), rmsnorm(mem, norm_mem[i]),
                                    w_xq[i], w_xkv[i], w_xo[i])
        h = h + swiglu(rmsnorm(h, norm_ffn[i]), w_ffn_in[i], w_ffn_out[i])
    return rmsnorm(h, norm_final)
```

```python
import functools

import jax
import jax.numpy as jnp
from jax import lax
from jax.experimental import pallas as pl
from jax.experimental.pallas import tpu as pltpu

F32 = jnp.float32
BF16 = jnp.bfloat16

D_MODEL = 1024
EPS = 1e-6
CHUNK = 64
SSD_D_INNER = 1024
SSD_HEAD_DIM = 64
SSD_HEADS = 16
SSD_GROUPS = 2
SSD_HEADS_PER_GROUP = SSD_HEADS // SSD_GROUPS
SSD_STATE = 128
SSD_CONV = 4
SSD_CONV_CH = SSD_D_INNER + 2 * SSD_GROUPS * SSD_STATE
GLA_HEADS = 4
GLA_DK = 512
GLA_DV = 1024
GLA_HEAD_K = GLA_DK // GLA_HEADS
GLA_HEAD_V = GLA_DV // GLA_HEADS
GLA_GATE_RANK = 16
GLA_TAU = 16.0
XATTN_HEADS = 4
XATTN_HEAD_DIM = D_MODEL // XATTN_HEADS
D_FF = 2816
IN_SIZES = (SSD_D_INNER, SSD_CONV_CH, SSD_HEADS, GLA_DK, GLA_DK, GLA_DV, GLA_DV, GLA_GATE_RANK,
            D_MODEL, D_MODEL)

LANES = 128
SUBLANES = 8
VMEM_LIMIT_BYTES = 56 * 1024 * 1024

OFF_Z = 0
OFF_XBC = OFF_Z + SSD_D_INNER
OFF_Q = OFF_XBC + SSD_CONV_CH
OFF_K = OFF_Q + GLA_DK
OFF_V = OFF_K + GLA_DK
OFF_R = OFF_V + GLA_DV
OFF_GS = OFF_R + GLA_DV
OFF_GG = OFF_GS + D_MODEL
OFF_SMALL = OFF_GG + D_MODEL
IN_WIDTH_PADDED = OFF_SMALL + LANES
SMALL_DT = 0
SMALL_A1 = SSD_HEADS

CONV_HALO = SUBLANES


def _dot(a, b):
    return jnp.dot(a, b, preferred_element_type=F32)


def _dot_nt(a, b):
    return lax.dot_general(a, b, (((1,), (1,)), ((), ())), preferred_element_type=F32)


def _dot_tn(a, b):
    return lax.dot_general(a, b, (((0,), (0,)), ((), ())), preferred_element_type=F32)


def _rms(x, g):
    return x * lax.rsqrt(jnp.mean(x * x, axis=-1, keepdims=True) + EPS) * g


def _silu(x):
    return x * jax.nn.sigmoid(x)


def _chunk_cumsum(tri, x):
    hi = x.astype(BF16)
    r1 = x - hi.astype(F32)
    mid = r1.astype(BF16)
    lo = (r1 - mid.astype(F32)).astype(BF16)
    return _dot(tri, hi) + _dot(tri, mid) + _dot(tri, lo)


def _mixer_kernel(x_ref, gmix_ref, win_ref, convw_ref, convb_ref, dtb_ref, alog_ref, dskip_ref,
                  ssdnorm_ref, wa2_ref, ba_ref, glanorm_ref, wus_ref, wug_ref, wo_ref,
                  h_ref,
                  n_ref, z_ref, xpad_ref, q_ref, k_ref, v_ref, r_ref, gs_ref, gg_ref,
                  xs_ref, b_ref, c_ref, dt_ref, a_ref, la_ref, y_ref, ys_ref, yg_ref,
                  sstate_ref, gstate_ref, *, tile):
    s = pl.program_id(1)

    @pl.when(s == 0)
    def _():
        sstate_ref[...] = jnp.zeros_like(sstate_ref)
        gstate_ref[...] = jnp.zeros_like(gstate_ref)
        xpad_ref[0:CONV_HALO, :] = jnp.zeros((CONV_HALO, SSD_CONV_CH), F32)

    n_ref[...] = _rms(x_ref[...], gmix_ref[...]).astype(BF16)

    def proj(off, width):
        return _dot(n_ref[...], win_ref[:, off:off + width])

    z_ref[...] = proj(OFF_Z, SSD_D_INNER)
    xpad_ref[CONV_HALO:CONV_HALO + tile, :] = proj(OFF_XBC, SSD_CONV_CH)
    q_ref[...] = proj(OFF_Q, GLA_DK)
    k_ref[...] = proj(OFF_K, GLA_DK)
    v_ref[...] = proj(OFF_V, GLA_DV).astype(BF16)
    r_ref[...] = proj(OFF_R, GLA_DV)
    gs_ref[...] = proj(OFF_GS, D_MODEL)
    gg_ref[...] = proj(OFF_GG, D_MODEL)
    small = proj(OFF_SMALL, LANES)

    dt = jax.nn.softplus(small + dtb_ref[...])
    dt_ref[...] = dt
    a_ref[...] = dt * (-jnp.exp(alog_ref[...]))
    logits = _dot(small.astype(BF16), wa2_ref[...]) + ba_ref[...]
    la_ref[...] = jax.nn.log_sigmoid(logits) / GLA_TAU

    col_block = 512
    for cb in range(SSD_CONV_CH // col_block):
        cols = slice(cb * col_block, (cb + 1) * col_block)
        acc = convb_ref[:, cols]
        for j in range(SSD_CONV):
            lo = CONV_HALO - (SSD_CONV - 1) + j
            acc = acc + convw_ref[j:j + 1, cols] * xpad_ref[lo:lo + tile, cols]
        act = _silu(acc)
        if cb < SSD_D_INNER // col_block:
            xs_ref[:, cols] = act
        else:
            bc = act.astype(BF16)
            half = col_block // 2
            b_ref[...] = bc[:, :half]
            c_ref[...] = bc[:, half:]
    xpad_ref[0:CONV_HALO, :] = xpad_ref[tile:tile + CONV_HALO, :]

    rid = lax.broadcasted_iota(jnp.int32, (CHUNK, CHUNK), 0)
    cid = lax.broadcasted_iota(jnp.int32, (CHUNK, CHUNK), 1)
    causal = rid >= cid
    tri = causal.astype(BF16)
    gn = SSD_D_INNER // SSD_GROUPS

    def chunk_body(c, carry):
        rows = pl.ds(pl.multiple_of(c * CHUNK, CHUNK), CHUNK)

        acs = _chunk_cumsum(tri, a_ref[rows, :])
        acs_t = acs.T
        dtc = dt_ref[rows, :]
        for g in range(SSD_GROUPS):
            nsl = slice(g * SSD_STATE, (g + 1) * SSD_STATE)
            bg = b_ref[rows, nsl]
            cg = c_ref[rows, nsl]
            cb_mat = _dot_nt(cg, bg)
            for r in range(SSD_HEADS_PER_GROUP):
                h = g * SSD_HEADS_PER_GROUP + r
                psl = slice(h * SSD_HEAD_DIM, (h + 1) * SSD_HEAD_DIM)
                col = acs[:, h:h + 1]
                row = acs_t[h:h + 1, :]
                last = acs[CHUNK - 1:CHUNK, h:h + 1]
                decay = jnp.where(causal, jnp.exp(jnp.where(causal, col - row, 0.0)), 0.0)
                xh = xs_ref[rows, psl]
                xdt = xh * dtc[:, h:h + 1]
                y_diag = _dot((cb_mat * decay).astype(BF16), xdt.astype(BF16))
                state = sstate_ref[psl, :]
                y_off = _dot_nt(cg, state.astype(BF16)) * jnp.exp(col)
                new = _dot_tn((xdt * jnp.exp(last - col)).astype(BF16), bg)
                sstate_ref[psl, :] = state * jnp.exp(last) + new
                y_ref[rows, psl] = y_diag + y_off + dskip_ref[:, psl] * xh
        yz = y_ref[rows, :] * _silu(z_ref[rows, :])
        for g in range(SSD_GROUPS):
            csl = slice(g * gn, (g + 1) * gn)
            ys_ref[rows, csl] = _rms(yz[:, csl], ssdnorm_ref[:, csl]).astype(BF16)

        bcum = _chunk_cumsum(tri, la_ref[rows, :])
        blast = bcum[CHUNK - 1:CHUNK, :]
        kc = k_ref[rows, :]
        q_t = (q_ref[rows, :] * (GLA_HEAD_K ** -0.5) * jnp.exp(bcum)).astype(BF16)
        k_t = (kc * jnp.exp(-bcum)).astype(BF16)
        k_h = (kc * jnp.exp(blast - bcum)).astype(BF16)
        dec = jnp.exp(blast)
        for j in range(GLA_HEADS):
            ksl = slice(j * GLA_HEAD_K, (j + 1) * GLA_HEAD_K)
            vsl = slice(j * GLA_HEAD_V, (j + 1) * GLA_HEAD_V)
            att = jnp.where(causal, _dot_nt(q_t[:, ksl], k_t[:, ksl]), 0.0)
            vj = v_ref[rows, vsl]
            state = gstate_ref[vsl, :]
            o = _dot(att.astype(BF16), vj) + _dot_nt(q_t[:, ksl], state.astype(BF16))
            gstate_ref[vsl, :] = state * dec[:, ksl] + _dot_tn(vj, k_h[:, ksl])
            o = _rms(o, glanorm_ref[...])
            yg_ref[rows, vsl] = (o * _silu(r_ref[rows, vsl])).astype(BF16)
        return carry

    lax.fori_loop(0, tile // CHUNK, chunk_body, 0)

    merged = (jax.nn.sigmoid(gs_ref[...]) * _dot(ys_ref[...], wus_ref[...])
              + jax.nn.sigmoid(gg_ref[...]) * _dot(yg_ref[...], wug_ref[...]))
    h_ref[...] = x_ref[...] + _dot(merged.astype(BF16), wo_ref[...])


def _memkv_kernel(mem_ref, g_ref, w_ref, k_ref, v_ref):
    m = _rms(mem_ref[...], g_ref[...]).astype(BF16)
    kv = _dot(m, w_ref[...])
    k_ref[...] = kv[:, :D_MODEL].astype(BF16)
    v_ref[...] = kv[:, D_MODEL:].astype(BF16)


def _tail_kernel(h_ref, gx_ref, wq_ref, k_ref, v_ref, wxo_ref, gf_ref, wfi_ref, wfo_ref, gfin_ref,
                 out_ref, o_ref, act_ref):
    h1 = h_ref[...]
    q = _dot(_rms(h1, gx_ref[...]).astype(BF16), wq_ref[...])
    for j in range(XATTN_HEADS):
        sl = slice(j * XATTN_HEAD_DIM, (j + 1) * XATTN_HEAD_DIM)
        sc = _dot_nt(q[:, sl].astype(BF16), k_ref[:, sl]) * (XATTN_HEAD_DIM ** -0.5)
        e = jnp.exp(sc - jnp.max(sc, axis=-1, keepdims=True))
        p = e / jnp.sum(e, axis=-1, keepdims=True)
        o_ref[:, sl] = _dot(p.astype(BF16), v_ref[:, sl]).astype(BF16)
    h2 = h1 + _dot(o_ref[...], wxo_ref[...])
    n3 = _rms(h2, gf_ref[...]).astype(BF16)
    ff_block = D_FF // 2
    for cb in range(D_FF // ff_block):
        gate = _dot(n3, wfi_ref[:, cb * ff_block:(cb + 1) * ff_block])
        up = _dot(n3, wfi_ref[:, D_FF + cb * ff_block:D_FF + (cb + 1) * ff_block])
        act_ref[:, cb * ff_block:(cb + 1) * ff_block] = (_silu(gate) * up).astype(BF16)
    h3 = h2 + _dot(act_ref[...], wfo_ref[...])
    out_ref[...] = _rms(h3, gfin_ref[...])


def _resident(shape):
    return pl.BlockSpec(shape, lambda b, s: (0,) * len(shape), pipeline_mode=pl.Buffered(1))


def _row(v, width=None):
    v = v.reshape(1, -1).astype(F32)
    if width is not None and v.shape[1] < width:
        v = jnp.pad(v, ((0, 0), (0, width - v.shape[1])))
    return v


@functools.partial(jax.jit, static_argnames=("tile",))
def _forward(x, mem, norm_mix, w_in, ssd_conv_w, ssd_conv_b, ssd_dt_bias, ssd_A_log, ssd_D, ssd_norm,
             gla_w_a2, gla_b_a, gla_norm, w_up_ssd, w_up_gla, w_o, norm_xattn, norm_mem, w_xq, w_xkv,
             w_xo, norm_ffn, w_ffn_in, w_ffn_out, norm_final, *, tile):
    batch, seq, _ = x.shape
    mem_len = mem.shape[1]
    assert seq % tile == 0 and tile % CHUNK == 0
    grid = (batch, seq // tile)

    bounds = [0]
    for sz in IN_SIZES:
        bounds.append(bounds[-1] + sz)
    w = w_in[0]
    sec = lambda i: w[:, bounds[i]:bounds[i + 1]]
    pad = jnp.zeros((D_MODEL, LANES - SSD_HEADS - GLA_GATE_RANK), w.dtype)
    w_in_r = jnp.concatenate([sec(0), sec(1), sec(3), sec(4), sec(5), sec(6), sec(8), sec(9),
                              sec(2), sec(7), pad], axis=1).astype(BF16)
    wa2 = jnp.zeros((LANES, GLA_DK), F32).at[SMALL_A1:SMALL_A1 + GLA_GATE_RANK].set(gla_w_a2[0]).astype(BF16)
    dskip = jnp.repeat(ssd_D[0].astype(F32), SSD_HEAD_DIM).reshape(1, SSD_D_INNER)

    tok_spec = pl.BlockSpec((None, tile, D_MODEL), lambda b, s: (b, s, 0))

    mixer_inputs = [
        (x, tok_spec),
        (_row(norm_mix[0]), _resident((1, D_MODEL))),
        (w_in_r, _resident((D_MODEL, IN_WIDTH_PADDED))),
        (ssd_conv_w[0].reshape(SSD_CONV, SSD_CONV_CH), _resident((SSD_CONV, SSD_CONV_CH))),
        (_row(ssd_conv_b[0]), _resident((1, SSD_CONV_CH))),
        (_row(ssd_dt_bias[0], LANES), _resident((1, LANES))),
        (_row(ssd_A_log[0], LANES), _resident((1, LANES))),
        (dskip, _resident((1, SSD_D_INNER))),
        (_row(ssd_norm[0]), _resident((1, SSD_D_INNER))),
        (wa2, _resident((LANES, GLA_DK))),
        (_row(gla_b_a[0]), _resident((1, GLA_DK))),
        (_row(gla_norm[0]), _resident((1, GLA_HEAD_V))),
        (w_up_ssd[0].astype(BF16), _resident((SSD_D_INNER, D_MODEL))),
        (w_up_gla[0].astype(BF16), _resident((GLA_DV, D_MODEL))),
        (w_o[0].astype(BF16), _resident((D_MODEL, D_MODEL))),
    ]
    mixer_scratch = [
        pltpu.VMEM((tile, D_MODEL), BF16),
        pltpu.VMEM((tile, SSD_D_INNER), F32),
        pltpu.VMEM((tile + CONV_HALO, SSD_CONV_CH), F32),
        pltpu.VMEM((tile, GLA_DK), F32),
        pltpu.VMEM((tile, GLA_DK), F32),
        pltpu.VMEM((tile, GLA_DV), BF16),
        pltpu.VMEM((tile, GLA_DV), F32),
        pltpu.VMEM((tile, D_MODEL), F32),
        pltpu.VMEM((tile, D_MODEL), F32),
        pltpu.VMEM((tile, SSD_D_INNER), F32),
        pltpu.VMEM((tile, SSD_GROUPS * SSD_STATE), BF16),
        pltpu.VMEM((tile, SSD_GROUPS * SSD_STATE), BF16),
        pltpu.VMEM((tile, LANES), F32),
        pltpu.VMEM((tile, LANES), F32),
        pltpu.VMEM((tile, GLA_DK), F32),
        pltpu.VMEM((tile, SSD_D_INNER), F32),
        pltpu.VMEM((tile, SSD_D_INNER), BF16),
        pltpu.VMEM((tile, GLA_DV), BF16),
        pltpu.VMEM((SSD_D_INNER, SSD_STATE), F32),
        pltpu.VMEM((GLA_DV, GLA_HEAD_K), F32),
    ]
    h1 = pl.pallas_call(
        functools.partial(_mixer_kernel, tile=tile),
        grid=grid,
        in_specs=[spec for _, spec in mixer_inputs],
        out_specs=tok_spec,
        out_shape=jax.ShapeDtypeStruct(x.shape, F32),
        scratch_shapes=mixer_scratch,
        compiler_params=pltpu.CompilerParams(
            dimension_semantics=("arbitrary", "arbitrary"), vmem_limit_bytes=VMEM_LIMIT_BYTES),
        name="mixer",
    )(*[a for a, _ in mixer_inputs])

    kv_shape = jax.ShapeDtypeStruct((batch, mem_len, D_MODEL), BF16)
    mem_spec = pl.BlockSpec((None, mem_len, D_MODEL), lambda b: (b, 0, 0))
    k_mem, v_mem = pl.pallas_call(
        _memkv_kernel,
        grid=(batch,),
        in_specs=[mem_spec,
                  pl.BlockSpec((1, D_MODEL), lambda b: (0, 0)),
                  pl.BlockSpec((D_MODEL, 2 * D_MODEL), lambda b: (0, 0))],
        out_specs=[mem_spec, mem_spec],
        out_shape=[kv_shape, kv_shape],
        compiler_params=pltpu.CompilerParams(
            dimension_semantics=("arbitrary",), vmem_limit_bytes=VMEM_LIMIT_BYTES),
        name="memkv",
    )(mem, _row(norm_mem[0]), w_xkv[0].astype(BF16))

    kvb_spec = pl.BlockSpec((None, mem_len, D_MODEL), lambda b, s: (b, 0, 0))
    out = pl.pallas_call(
        _tail_kernel,
        grid=grid,
        in_specs=[tok_spec,
                  _resident((1, D_MODEL)),
                  _resident((D_MODEL, D_MODEL)),
                  kvb_spec, kvb_spec,
                  _resident((D_MODEL, D_MODEL)),
                  _resident((1, D_MODEL)),
                  _resident((D_MODEL, 2 * D_FF)),
                  _resident((D_FF, D_MODEL)),
                  _resident((1, D_MODEL))],
        out_specs=tok_spec,
        out_shape=jax.ShapeDtypeStruct(x.shape, F32),
        scratch_shapes=[pltpu.VMEM((tile, D_MODEL), BF16),
                        pltpu.VMEM((tile, D_FF), BF16)],
        compiler_params=pltpu.CompilerParams(
            dimension_semantics=("arbitrary", "arbitrary"), vmem_limit_bytes=VMEM_LIMIT_BYTES),
        name="tail",
    )(h1, _row(norm_xattn[0]), w_xq[0].astype(BF16), k_mem, v_mem, w_xo[0].astype(BF16),
      _row(norm_ffn[0]), w_ffn_in[0].astype(BF16), w_ffn_out[0].astype(BF16), _row(norm_final))
    return out


def kernel(x, mem, norm_mix, w_in, ssd_conv_w, ssd_conv_b, ssd_dt_bias, ssd_A_log, ssd_D, ssd_norm,
           gla_w_a2, gla_b_a, gla_norm, w_up_ssd, w_up_gla, w_o, norm_xattn, norm_mem, w_xq, w_xkv,
           w_xo, norm_ffn, w_ffn_in, w_ffn_out, norm_final):
    return _forward(x, mem, norm_mix, w_in, ssd_conv_w, ssd_conv_b, ssd_dt_bias, ssd_A_log, ssd_D,
                    ssd_norm, gla_w_a2, gla_b_a, gla_norm, w_up_ssd, w_up_gla, w_o, norm_xattn,
                    norm_mem, w_xq, w_xkv, w_xo, norm_ffn, w_ffn_in, w_ffn_out, norm_final, tile=256)
```

```python
import functools

import jax
import jax.numpy as jnp
from jax import lax
from jax.experimental import pallas as pl
from jax.experimental.pallas import tpu as pltpu

F32 = jnp.float32
BF16 = jnp.bfloat16

D_MODEL = 1024
EPS = 1e-6
CHUNK = 64
SSD_D_INNER = 1024
SSD_HEAD_DIM = 64
SSD_HEADS = 16
SSD_GROUPS = 2
SSD_HEADS_PER_GROUP = SSD_HEADS // SSD_GROUPS
SSD_STATE = 128
SSD_CONV = 4
SSD_CONV_CH = SSD_D_INNER + 2 * SSD_GROUPS * SSD_STATE
GLA_HEADS = 4
GLA_DK = 512
GLA_DV = 1024
GLA_HEAD_K = GLA_DK // GLA_HEADS
GLA_HEAD_V = GLA_DV // GLA_HEADS
GLA_GATE_RANK = 16
GLA_TAU = 16.0
XATTN_HEADS = 4
XATTN_HEAD_DIM = D_MODEL // XATTN_HEADS
D_FF = 2816
IN_SIZES = (SSD_D_INNER, SSD_CONV_CH, SSD_HEADS, GLA_DK, GLA_DK, GLA_DV, GLA_DV, GLA_GATE_RANK,
            D_MODEL, D_MODEL)

LANES = 128
SUBLANES = 8
VMEM_LIMIT_BYTES = 56 * 1024 * 1024

OFF_Z = 0
OFF_XBC = OFF_Z + SSD_D_INNER
OFF_Q = OFF_XBC + SSD_CONV_CH
OFF_K = OFF_Q + GLA_DK
OFF_V = OFF_K + GLA_DK
OFF_R = OFF_V + GLA_DV
OFF_GS = OFF_R + GLA_DV
OFF_GG = OFF_GS + D_MODEL
OFF_SMALL = OFF_GG + D_MODEL
IN_WIDTH_PADDED = OFF_SMALL + LANES
SMALL_DT = 0
SMALL_A1 = SSD_HEADS

CONV_HALO = SUBLANES


def _dot(a, b):
    return jnp.dot(a, b, preferred_element_type=F32)


def _dot_nt(a, b):
    return lax.dot_general(a, b, (((1,), (1,)), ((), ())), preferred_element_type=F32)


def _dot_tn(a, b):
    return lax.dot_general(a, b, (((0,), (0,)), ((), ())), preferred_element_type=F32)


def _rms(x, g):
    return x * lax.rsqrt(jnp.mean(x * x, axis=-1, keepdims=True) + EPS) * g


def _silu(x):
    return x * jax.nn.sigmoid(x)


def _split3(x):
    hi = x.astype(BF16)
    r1 = x - hi.astype(F32)
    mid = r1.astype(BF16)
    lo = (r1 - mid.astype(F32)).astype(BF16)
    return hi, mid, lo


def _cumsum_rows(tri, x):
    hi, mid, lo = _split3(x)
    return _dot(tri, hi) + _dot(tri, mid) + _dot(tri, lo)


def _cumsum_lanes(x, upper):
    hi, mid, lo = _split3(x)
    return _dot(hi, upper) + _dot(mid, upper) + _dot(lo, upper)


def _mixer_kernel(x_ref, gmix_ref, win_ref, convw_ref, convb_ref, dtb_ref, alog_ref, dskip_ref,
                  ssdnorm_ref, wa2_ref, ba_ref, glanorm_ref, wus_ref, wug_ref, wo_ref,
                  h_ref,
                  n_ref, z_ref, xpad_ref, q_ref, k_ref, v_ref, r_ref, gs_ref, gg_ref,
                  xs_ref, b_ref, c_ref, dt_ref, a_ref, la_ref, yt_ref, xd_ref, ys_ref,
                  qt_ref, kt_ref, kh_ref, dec_ref, o_ref, yg_ref,
                  sstate_ref, gstate_ref, *, tile):
    s = pl.program_id(1)

    @pl.when(s == 0)
    def _():
        sstate_ref[...] = jnp.zeros_like(sstate_ref)
        gstate_ref[...] = jnp.zeros_like(gstate_ref)
        xpad_ref[0:CONV_HALO, :] = jnp.zeros((CONV_HALO, SSD_CONV_CH), F32)

    n_ref[...] = _rms(x_ref[...], gmix_ref[...]).astype(BF16)

    def proj(off, width):
        return _dot(n_ref[...], win_ref[:, off:off + width])

    z_ref[...] = proj(OFF_Z, SSD_D_INNER)
    xpad_ref[CONV_HALO:CONV_HALO + tile, :] = proj(OFF_XBC, SSD_CONV_CH)
    q_ref[...] = proj(OFF_Q, GLA_DK)
    k_ref[...] = proj(OFF_K, GLA_DK)
    v_ref[...] = proj(OFF_V, GLA_DV).astype(BF16)
    r_ref[...] = proj(OFF_R, GLA_DV)
    gs_ref[...] = proj(OFF_GS, D_MODEL)
    gg_ref[...] = proj(OFF_GG, D_MODEL)
    small = proj(OFF_SMALL, LANES)

    dt = jax.nn.softplus(small + dtb_ref[...])
    dt_ref[...] = dt
    a_ref[...] = dt * (-jnp.exp(alog_ref[...]))
    logits = _dot(small.astype(BF16), wa2_ref[...]) + ba_ref[...]
    la_ref[...] = jax.nn.log_sigmoid(logits) / GLA_TAU

    col_block = 512
    for cb in range(SSD_CONV_CH // col_block):
        cols = slice(cb * col_block, (cb + 1) * col_block)
        acc = convb_ref[:, cols]
        for j in range(SSD_CONV):
            lo = CONV_HALO - (SSD_CONV - 1) + j
            acc = acc + convw_ref[j:j + 1, cols] * xpad_ref[lo:lo + tile, cols]
        act = _silu(acc)
        if cb < SSD_D_INNER // col_block:
            xs_ref[:, cols] = act
        else:
            bc = act.astype(BF16)
            half = col_block // 2
            b_ref[...] = bc[:, :half]
            c_ref[...] = bc[:, half:]
    xpad_ref[0:CONV_HALO, :] = xpad_ref[tile:tile + CONV_HALO, :]

    rid = lax.broadcasted_iota(jnp.int32, (tile, tile), 0)
    cid = lax.broadcasted_iota(jnp.int32, (tile, tile), 1)
    upper = rid <= cid
    a_t = a_ref[...].T
    acs_t = _cumsum_lanes(a_t, upper.astype(BF16))
    acs = acs_t.T
    dt_t = dt_ref[...].T
    xs_t = xs_ref[...].T
    hp = SSD_HEADS_PER_GROUP * SSD_HEAD_DIM
    for g in range(SSD_GROUPS):
        nsl = slice(g * SSD_STATE, (g + 1) * SSD_STATE)
        bg = b_ref[:, nsl]
        cg = c_ref[:, nsl]
        cb_t = jnp.where(upper, _dot_nt(bg, cg), 0.0)
        state = sstate_ref[g * hp:(g + 1) * hp, :]
        y_off_t = _dot_nt(state.astype(BF16), cg)
        for r in range(SSD_HEADS_PER_GROUP):
            h = g * SSD_HEADS_PER_GROUP + r
            psl = slice(h * SSD_HEAD_DIM, (h + 1) * SSD_HEAD_DIM)
            rsl = slice(r * SSD_HEAD_DIM, (r + 1) * SSD_HEAD_DIM)
            col = acs[:, h:h + 1]
            row = acs_t[h:h + 1, :]
            last = row[:, tile - 1:tile]
            m_t = (jnp.exp(jnp.minimum(row - col, 0.0)) * cb_t).astype(BF16)
            xh_t = xs_t[psl, :]
            xdt_t = xh_t * dt_t[h:h + 1, :]
            y_diag_t = _dot(xdt_t.astype(BF16), m_t)
            yt_ref[psl, :] = (y_diag_t + y_off_t[rsl, :] * jnp.exp(row)
                              + dskip_ref[h:h + 1, :] * xh_t)
            xd_ref[rsl, :] = (xdt_t * jnp.exp(last - row)).astype(BF16)
            sstate_ref[psl, :] = state[rsl, :] * jnp.exp(last)
        sstate_ref[g * hp:(g + 1) * hp, :] += _dot(xd_ref[...], bg)
    yz = yt_ref[...].T * _silu(z_ref[...])
    gn = SSD_D_INNER // SSD_GROUPS
    for g in range(SSD_GROUPS):
        csl = slice(g * gn, (g + 1) * gn)
        ys_ref[:, csl] = _rms(yz[:, csl], ssdnorm_ref[:, csl]).astype(BF16)

    n_chunks = tile // CHUNK
    chunk_of = lambda idx: lax.shift_right_logical(idx, CHUNK.bit_length() - 1)
    tri_blocks = ((rid >= cid) & (chunk_of(rid) == chunk_of(cid))).astype(BF16)
    bcum = _cumsum_rows(tri_blocks, la_ref[...])
    blast = jnp.concatenate(
        [jnp.broadcast_to(bcum[(i + 1) * CHUNK - 1:(i + 1) * CHUNK, :], (CHUNK, GLA_DK))
         for i in range(n_chunks)], axis=0)
    kk = k_ref[...]
    qt_ref[...] = (q_ref[...] * (GLA_HEAD_K ** -0.5) * jnp.exp(bcum)).astype(BF16)
    kt_ref[...] = (kk * jnp.exp(-bcum)).astype(BF16)
    kh_ref[...] = (kk * jnp.exp(blast - bcum)).astype(BF16)
    dec_ref[...] = jnp.exp(blast)
    causal = (lax.broadcasted_iota(jnp.int32, (CHUNK, CHUNK), 0)
              >= lax.broadcasted_iota(jnp.int32, (CHUNK, CHUNK), 1))
    for j in range(GLA_HEADS):
        ksl = slice(j * GLA_HEAD_K, (j + 1) * GLA_HEAD_K)
        vsl = slice(j * GLA_HEAD_V, (j + 1) * GLA_HEAD_V)
        state = gstate_ref[vsl, :]
        for i in range(n_chunks):
            rs = slice(i * CHUNK, (i + 1) * CHUNK)
            q_c = qt_ref[rs, ksl]
            att = jnp.where(causal, _dot_nt(q_c, kt_ref[rs, ksl]), 0.0)
            vj = v_ref[rs, vsl]
            o_ref[rs, vsl] = _dot(att.astype(BF16), vj) + _dot_nt(q_c, state.astype(BF16))
            state = state * dec_ref[i * CHUNK:i * CHUNK + 1, ksl] + _dot_tn(vj, kh_ref[rs, ksl])
        gstate_ref[vsl, :] = state
        yg_ref[:, vsl] = (_rms(o_ref[:, vsl], glanorm_ref[...]) * _silu(r_ref[:, vsl])).astype(BF16)

    merged = (jax.nn.sigmoid(gs_ref[...]) * _dot(ys_ref[...], wus_ref[...])
              + jax.nn.sigmoid(gg_ref[...]) * _dot(yg_ref[...], wug_ref[...]))
    h_ref[...] = x_ref[...] + _dot(merged.astype(BF16), wo_ref[...])


def _memkv_kernel(mem_ref, g_ref, w_ref, k_ref, v_ref):
    m = _rms(mem_ref[...], g_ref[...]).astype(BF16)
    kv = _dot(m, w_ref[...])
    k_ref[...] = kv[:, :D_MODEL].astype(BF16)
    v_ref[...] = kv[:, D_MODEL:].astype(BF16)


def _tail_kernel(h_ref, gx_ref, wq_ref, k_ref, v_ref, wxo_ref, gf_ref, wfi_ref, wfo_ref, gfin_ref,
                 out_ref, o_ref, act_ref):
    h1 = h_ref[...]
    q = _dot(_rms(h1, gx_ref[...]).astype(BF16), wq_ref[...])
    for j in range(XATTN_HEADS):
        sl = slice(j * XATTN_HEAD_DIM, (j + 1) * XATTN_HEAD_DIM)
        sc = _dot_nt(q[:, sl].astype(BF16), k_ref[:, sl]) * (XATTN_HEAD_DIM ** -0.5)
        e = jnp.exp(sc - jnp.max(sc, axis=-1, keepdims=True))
        p = e / jnp.sum(e, axis=-1, keepdims=True)
        o_ref[:, sl] = _dot(p.astype(BF16), v_ref[:, sl]).astype(BF16)
    h2 = h1 + _dot(o_ref[...], wxo_ref[...])
    n3 = _rms(h2, gf_ref[...]).astype(BF16)
    ff_block = D_FF // 2
    for cb in range(D_FF // ff_block):
        gate = _dot(n3, wfi_ref[:, cb * ff_block:(cb + 1) * ff_block])
        up = _dot(n3, wfi_ref[:, D_FF + cb * ff_block:D_FF + (cb + 1) * ff_block])
        act_ref[:, cb * ff_block:(cb + 1) * ff_block] = (_silu(gate) * up).astype(BF16)
    h3 = h2 + _dot(act_ref[...], wfo_ref[...])
    out_ref[...] = _rms(h3, gfin_ref[...])


def _resident(shape):
    return pl.BlockSpec(shape, lambda b, s: (0,) * len(shape), pipeline_mode=pl.Buffered(1))


def _row(v, width=None):
    v = v.reshape(1, -1).astype(F32)
    if width is not None and v.shape[1] < width:
        v = jnp.pad(v, ((0, 0), (0, width - v.shape[1])))
    return v


@functools.partial(jax.jit, static_argnames=("tile",))
def _forward(x, mem, norm_mix, w_in, ssd_conv_w, ssd_conv_b, ssd_dt_bias, ssd_A_log, ssd_D, ssd_norm,
             gla_w_a2, gla_b_a, gla_norm, w_up_ssd, w_up_gla, w_o, norm_xattn, norm_mem, w_xq, w_xkv,
             w_xo, norm_ffn, w_ffn_in, w_ffn_out, norm_final, *, tile):
    batch, seq, _ = x.shape
    mem_len = mem.shape[1]
    assert seq % tile == 0 and tile % CHUNK == 0
    grid = (batch, seq // tile)

    bounds = [0]
    for sz in IN_SIZES:
        bounds.append(bounds[-1] + sz)
    w = w_in[0]
    sec = lambda i: w[:, bounds[i]:bounds[i + 1]]
    pad = jnp.zeros((D_MODEL, LANES - SSD_HEADS - GLA_GATE_RANK), w.dtype)
    w_in_r = jnp.concatenate([sec(0), sec(1), sec(3), sec(4), sec(5), sec(6), sec(8), sec(9),
                              sec(2), sec(7), pad], axis=1).astype(BF16)
    wa2 = jnp.zeros((LANES, GLA_DK), F32).at[SMALL_A1:SMALL_A1 + GLA_GATE_RANK].set(gla_w_a2[0]).astype(BF16)
    dskip = jnp.broadcast_to(ssd_D[0].astype(F32)[:, None], (SSD_HEADS, tile))

    tok_spec = pl.BlockSpec((None, tile, D_MODEL), lambda b, s: (b, s, 0))

    mixer_inputs = [
        (x, tok_spec),
        (_row(norm_mix[0]), _resident((1, D_MODEL))),
        (w_in_r, _resident((D_MODEL, IN_WIDTH_PADDED))),
        (ssd_conv_w[0].reshape(SSD_CONV, SSD_CONV_CH), _resident((SSD_CONV, SSD_CONV_CH))),
        (_row(ssd_conv_b[0]), _resident((1, SSD_CONV_CH))),
        (_row(ssd_dt_bias[0], LANES), _resident((1, LANES))),
        (_row(ssd_A_log[0], LANES), _resident((1, LANES))),
        (dskip, _resident((SSD_HEADS, tile))),
        (_row(ssd_norm[0]), _resident((1, SSD_D_INNER))),
        (wa2, _resident((LANES, GLA_DK))),
        (_row(gla_b_a[0]), _resident((1, GLA_DK))),
        (_row(gla_norm[0]), _resident((1, GLA_HEAD_V))),
        (w_up_ssd[0].astype(BF16), _resident((SSD_D_INNER, D_MODEL))),
        (w_up_gla[0].astype(BF16), _resident((GLA_DV, D_MODEL))),
        (w_o[0].astype(BF16), _resident((D_MODEL, D_MODEL))),
    ]
    mixer_scratch = [
        pltpu.VMEM((tile, D_MODEL), BF16),
        pltpu.VMEM((tile, SSD_D_INNER), F32),
        pltpu.VMEM((tile + CONV_HALO, SSD_CONV_CH), F32),
        pltpu.VMEM((tile, GLA_DK), F32),
        pltpu.VMEM((tile, GLA_DK), F32),
        pltpu.VMEM((tile, GLA_DV), BF16),
        pltpu.VMEM((tile, GLA_DV), F32),
        pltpu.VMEM((tile, D_MODEL), F32),
        pltpu.VMEM((tile, D_MODEL), F32),
        pltpu.VMEM((tile, SSD_D_INNER), F32),
        pltpu.VMEM((tile, SSD_GROUPS * SSD_STATE), BF16),
        pltpu.VMEM((tile, SSD_GROUPS * SSD_STATE), BF16),
        pltpu.VMEM((tile, LANES), F32),
        pltpu.VMEM((tile, LANES), F32),
        pltpu.VMEM((tile, GLA_DK), F32),
        pltpu.VMEM((SSD_D_INNER, tile), F32),
        pltpu.VMEM((SSD_HEADS_PER_GROUP * SSD_HEAD_DIM, tile), BF16),
        pltpu.VMEM((tile, SSD_D_INNER), BF16),
        pltpu.VMEM((tile, GLA_DK), BF16),
        pltpu.VMEM((tile, GLA_DK), BF16),
        pltpu.VMEM((tile, GLA_DK), BF16),
        pltpu.VMEM((tile, GLA_DK), F32),
        pltpu.VMEM((tile, GLA_DV), F32),
        pltpu.VMEM((tile, GLA_DV), BF16),
        pltpu.VMEM((SSD_D_INNER, SSD_STATE), F32),
        pltpu.VMEM((GLA_DV, GLA_HEAD_K), F32),
    ]
    h1 = pl.pallas_call(
        functools.partial(_mixer_kernel, tile=tile),
        grid=grid,
        in_specs=[spec for _, spec in mixer_inputs],
        out_specs=tok_spec,
        out_shape=jax.ShapeDtypeStruct(x.shape, F32),
        scratch_shapes=mixer_scratch,
        compiler_params=pltpu.CompilerParams(
            dimension_semantics=("arbitrary", "arbitrary"), vmem_limit_bytes=VMEM_LIMIT_BYTES),
        name="mixer",
    )(*[a for a, _ in mixer_inputs])

    kv_shape = jax.ShapeDtypeStruct((batch, mem_len, D_MODEL), BF16)
    mem_spec = pl.BlockSpec((None, mem_len, D_MODEL), lambda b: (b, 0, 0))
    k_mem, v_mem = pl.pallas_call(
        _memkv_kernel,
        grid=(batch,),
        in_specs=[mem_spec,
                  pl.BlockSpec((1, D_MODEL), lambda b: (0, 0)),
                  pl.BlockSpec((D_MODEL, 2 * D_MODEL), lambda b: (0, 0))],
        out_specs=[mem_spec, mem_spec],
        out_shape=[kv_shape, kv_shape],
        compiler_params=pltpu.CompilerParams(
            dimension_semantics=("arbitrary",), vmem_limit_bytes=VMEM_LIMIT_BYTES),
        name="memkv",
    )(mem, _row(norm_mem[0]), w_xkv[0].astype(BF16))

    kvb_spec = pl.BlockSpec((None, mem_len, D_MODEL), lambda b, s: (b, 0, 0))
    out = pl.pallas_call(
        _tail_kernel,
        grid=grid,
        in_specs=[tok_spec,
                  _resident((1, D_MODEL)),
                  _resident((D_MODEL, D_MODEL)),
                  kvb_spec, kvb_spec,
                  _resident((D_MODEL, D_MODEL)),
                  _resident((1, D_MODEL)),
                  _resident((D_MODEL, 2 * D_FF)),
                  _resident((D_FF, D_MODEL)),
                  _resident((1, D_MODEL))],
        out_specs=tok_spec,
        out_shape=jax.ShapeDtypeStruct(x.shape, F32),
        scratch_shapes=[pltpu.VMEM((tile, D_MODEL), BF16),
                        pltpu.VMEM((tile, D_FF), BF16)],
        compiler_params=pltpu.CompilerParams(
            dimension_semantics=("arbitrary", "arbitrary"), vmem_limit_bytes=VMEM_LIMIT_BYTES),
        name="tail",
    )(h1, _row(norm_xattn[0]), w_xq[0].astype(BF16), k_mem, v_mem, w_xo[0].astype(BF16),
      _row(norm_ffn[0]), w_ffn_in[0].astype(BF16), w_ffn_out[0].astype(BF16), _row(norm_final))
    return out


def kernel(x, mem, norm_mix, w_in, ssd_conv_w, ssd_conv_b, ssd_dt_bias, ssd_A_log, ssd_D, ssd_norm,
           gla_w_a2, gla_b_a, gla_norm, w_up_ssd, w_up_gla, w_o, norm_xattn, norm_mem, w_xq, w_xkv,
           w_xo, norm_ffn, w_ffn_in, w_ffn_out, norm_final):
    return _forward(x, mem, norm_mix, w_in, ssd_conv_w, ssd_conv_b, ssd_dt_bias, ssd_A_log, ssd_D,
                    ssd_norm, gla_w_a2, gla_b_a, gla_norm, w_up_ssd, w_up_gla, w_o, norm_xattn,
                    norm_mem, w_xq, w_xkv, w_xo, norm_ffn, w_ffn_in, w_ffn_out, norm_final, tile=256)
```

```python
import functools

import jax
import jax.numpy as jnp
from jax import lax
from jax.experimental import pallas as pl
from jax.experimental.pallas import tpu as pltpu

F32 = jnp.float32
BF16 = jnp.bfloat16

D_MODEL = 1024
EPS = 1e-6
CHUNK = 64
SSD_D_INNER = 1024
SSD_HEAD_DIM = 64
SSD_HEADS = 16
SSD_GROUPS = 2
SSD_HEADS_PER_GROUP = SSD_HEADS // SSD_GROUPS
SSD_STATE = 128
SSD_CONV = 4
SSD_CONV_CH = SSD_D_INNER + 2 * SSD_GROUPS * SSD_STATE
GLA_HEADS = 4
GLA_DK = 512
GLA_DV = 1024
GLA_HEAD_K = GLA_DK // GLA_HEADS
GLA_HEAD_V = GLA_DV // GLA_HEADS
GLA_GATE_RANK = 16
GLA_TAU = 16.0
XATTN_HEADS = 4
XATTN_HEAD_DIM = D_MODEL // XATTN_HEADS
D_FF = 2816
IN_SIZES = (SSD_D_INNER, SSD_CONV_CH, SSD_HEADS, GLA_DK, GLA_DK, GLA_DV, GLA_DV, GLA_GATE_RANK,
            D_MODEL, D_MODEL)

LANES = 128
SUBLANES = 8
VMEM_LIMIT_BYTES = 56 * 1024 * 1024

OFF_Z = 0
OFF_XBC = OFF_Z + SSD_D_INNER
OFF_Q = OFF_XBC + SSD_CONV_CH
OFF_K = OFF_Q + GLA_DK
OFF_V = OFF_K + GLA_DK
OFF_R = OFF_V + GLA_DV
OFF_GS = OFF_R + GLA_DV
OFF_GG = OFF_GS + D_MODEL
OFF_SMALL = OFF_GG + D_MODEL
IN_WIDTH_PADDED = OFF_SMALL + LANES
SMALL_DT = 0
SMALL_A1 = SSD_HEADS

CONV_HALO = SUBLANES


def _dot(a, b):
    return jnp.dot(a, b, preferred_element_type=F32)


def _dot_nt(a, b):
    return lax.dot_general(a, b, (((1,), (1,)), ((), ())), preferred_element_type=F32)


def _dot_tn(a, b):
    return lax.dot_general(a, b, (((0,), (0,)), ((), ())), preferred_element_type=F32)


def _rms(x, g):
    return x * lax.rsqrt(jnp.mean(x * x, axis=-1, keepdims=True) + EPS) * g


def _silu(x):
    return x * jax.nn.sigmoid(x)


def _split3(x):
    hi = x.astype(BF16)
    r1 = x - hi.astype(F32)
    mid = r1.astype(BF16)
    lo = (r1 - mid.astype(F32)).astype(BF16)
    return hi, mid, lo


def _cumsum_rows(tri, x):
    hi, mid, lo = _split3(x)
    return _dot(tri, hi) + _dot(tri, mid) + _dot(tri, lo)


def _cumsum_lanes(x, upper):
    hi, mid, lo = _split3(x)
    return _dot(hi, upper) + _dot(mid, upper) + _dot(lo, upper)


def _mixer_kernel(x_ref, gmix_ref, win_ref, convw_ref, convb_ref, dtb_ref, alog_ref, dskip_ref,
                  ssdnorm_ref, wa2_ref, ba_ref, glanorm_ref, wus_ref, wug_ref, wo_ref,
                  h_ref,
                  n_ref, z_ref, xpad_ref, q_ref, k_ref, v_ref, r_ref, gs_ref, gg_ref,
                  xs_ref, b_ref, c_ref, dt_ref, a_ref, la_ref, yt_ref, xd_ref, ys_ref,
                  qt_ref, kt_ref, kh_ref, sb_ref, o_ref, yg_ref,
                  sstate_ref, gstate_ref, *, tile):
    s = pl.program_id(1)

    @pl.when(s == 0)
    def _():
        sstate_ref[...] = jnp.zeros_like(sstate_ref)
        gstate_ref[...] = jnp.zeros_like(gstate_ref)
        xpad_ref[0:CONV_HALO, :] = jnp.zeros((CONV_HALO, SSD_CONV_CH), F32)

    n_ref[...] = _rms(x_ref[...], gmix_ref[...]).astype(BF16)

    def proj(off, width):
        return _dot(n_ref[...], win_ref[:, off:off + width])

    slab = 512
    deferred = [(dst, off, c0)
                for dst, off, width in ((q_ref, OFF_Q, GLA_DK), (k_ref, OFF_K, GLA_DK),
                                        (v_ref, OFF_V, GLA_DV), (r_ref, OFF_R, GLA_DV),
                                        (z_ref, OFF_Z, SSD_D_INNER), (gs_ref, OFF_GS, D_MODEL),
                                        (gg_ref, OFF_GG, D_MODEL))
                for c0 in range(0, width, slab)]

    def emit_proj(count=1):
        for _ in range(count):
            if deferred:
                dst, off, c0 = deferred.pop(0)
                dst[:, c0:c0 + slab] = proj(off + c0, slab).astype(dst.dtype)

    xpad_ref[CONV_HALO:CONV_HALO + tile, :] = proj(OFF_XBC, SSD_CONV_CH)
    small = proj(OFF_SMALL, LANES)

    dt = jax.nn.softplus(small + dtb_ref[...])
    dt_ref[...] = dt
    a_ref[...] = dt * (-jnp.exp(alog_ref[...]))
    logits = _dot(small.astype(BF16), wa2_ref[...]) + ba_ref[...]
    la_ref[...] = jax.nn.log_sigmoid(logits) / GLA_TAU

    col_block = 512
    for cb in range(SSD_CONV_CH // col_block):
        cols = slice(cb * col_block, (cb + 1) * col_block)
        acc = convb_ref[:, cols]
        for j in range(SSD_CONV):
            lo = CONV_HALO - (SSD_CONV - 1) + j
            acc = acc + convw_ref[j:j + 1, cols] * xpad_ref[lo:lo + tile, cols]
        act = _silu(acc)
        if cb < SSD_D_INNER // col_block:
            xs_ref[:, cols] = act
        else:
            bc = act.astype(BF16)
            half = col_block // 2
            b_ref[...] = bc[:, :half]
            c_ref[...] = bc[:, half:]
        emit_proj()
    xpad_ref[0:CONV_HALO, :] = xpad_ref[tile:tile + CONV_HALO, :]

    rid = lax.broadcasted_iota(jnp.int32, (tile, tile), 0)
    cid = lax.broadcasted_iota(jnp.int32, (tile, tile), 1)
    upper = rid <= cid
    a_t = a_ref[...].T
    acs_t = _cumsum_lanes(a_t, upper.astype(BF16))
    acs = acs_t.T
    dt_t = dt_ref[...].T
    xs_t = xs_ref[...].T
    hp = SSD_HEADS_PER_GROUP * SSD_HEAD_DIM
    for g in range(SSD_GROUPS):
        nsl = slice(g * SSD_STATE, (g + 1) * SSD_STATE)
        bg = b_ref[:, nsl]
        cg = c_ref[:, nsl]
        cb_t = jnp.where(upper, _dot_nt(bg, cg), 0.0)
        state = sstate_ref[g * hp:(g + 1) * hp, :]
        y_off_t = _dot_nt(state.astype(BF16), cg)
        for r in range(SSD_HEADS_PER_GROUP):
            h = g * SSD_HEADS_PER_GROUP + r
            psl = slice(h * SSD_HEAD_DIM, (h + 1) * SSD_HEAD_DIM)
            rsl = slice(r * SSD_HEAD_DIM, (r + 1) * SSD_HEAD_DIM)
            col = acs[:, h:h + 1]
            row = acs_t[h:h + 1, :]
            last = row[:, tile - 1:tile]
            m_t = (jnp.exp(jnp.minimum(row - col, 0.0)) * cb_t).astype(BF16)
            xh_t = xs_t[psl, :]
            xdt_t = xh_t * dt_t[h:h + 1, :]
            y_diag_t = _dot(xdt_t.astype(BF16), m_t)
            yt_ref[psl, :] = (y_diag_t + y_off_t[rsl, :] * jnp.exp(row)
                              + dskip_ref[h:h + 1, :] * xh_t)
            xd_ref[rsl, :] = (xdt_t * jnp.exp(last - row)).astype(BF16)
            sstate_ref[psl, :] = state[rsl, :] * jnp.exp(last)
            emit_proj()
        sstate_ref[g * hp:(g + 1) * hp, :] += _dot(xd_ref[...], bg)
    emit_proj(len(deferred))
    yz = yt_ref[...].T * _silu(z_ref[...])
    gn = SSD_D_INNER // SSD_GROUPS
    for g in range(SSD_GROUPS):
        csl = slice(g * gn, (g + 1) * gn)
        ys_ref[:, csl] = _rms(yz[:, csl], ssdnorm_ref[:, csl]).astype(BF16)
    merged_ssd = jax.nn.sigmoid(gs_ref[...]) * _dot(ys_ref[...], wus_ref[...])

    n_chunks = tile // CHUNK
    chunk_of = lambda idx: lax.shift_right_logical(idx, CHUNK.bit_length() - 1)
    tri_blocks = ((rid >= cid) & (chunk_of(rid) == chunk_of(cid))).astype(BF16)
    bcum = _cumsum_rows(tri_blocks, la_ref[...])
    blast = jnp.concatenate(
        [jnp.broadcast_to(bcum[(i + 1) * CHUNK - 1:(i + 1) * CHUNK, :], (CHUNK, GLA_DK))
         for i in range(n_chunks)], axis=0)
    kk = k_ref[...]
    qt_ref[...] = (q_ref[...] * (GLA_HEAD_K ** -0.5) * jnp.exp(bcum)).astype(BF16)
    kt_ref[...] = (kk * jnp.exp(-bcum)).astype(BF16)
    kh_ref[...] = (kk * jnp.exp(blast - bcum)).astype(BF16)
    dec_t = jnp.exp(blast).T
    causal = (lax.broadcasted_iota(jnp.int32, (CHUNK, CHUNK), 0)
              >= lax.broadcasted_iota(jnp.int32, (CHUNK, CHUNK), 1))
    ksls = [slice(j * GLA_HEAD_K, (j + 1) * GLA_HEAD_K) for j in range(GLA_HEADS)]
    vsls = [slice(j * GLA_HEAD_V, (j + 1) * GLA_HEAD_V) for j in range(GLA_HEADS)]
    states = [gstate_ref[ksl, :] for ksl in ksls]
    for i in range(n_chunks):
        rs = slice(i * CHUNK, (i + 1) * CHUNK)
        for j in range(GLA_HEADS):
            sb_ref[i, ksls[j], :] = states[j].astype(BF16)
            update = _dot_tn(kh_ref[rs, ksls[j]], v_ref[rs, vsls[j]])
            states[j] = states[j] * dec_t[ksls[j], i * CHUNK:i * CHUNK + 1] + update
    for j in range(GLA_HEADS):
        gstate_ref[ksls[j], :] = states[j]
    for i in range(n_chunks):
        rs = slice(i * CHUNK, (i + 1) * CHUNK)
        atts = [jnp.where(causal, _dot_nt(qt_ref[rs, ksls[j]], kt_ref[rs, ksls[j]]), 0.0).astype(BF16)
                for j in range(GLA_HEADS)]
        for j in range(GLA_HEADS):
            o_ref[rs, vsls[j]] = (_dot(atts[j], v_ref[rs, vsls[j]])
                                  + _dot(qt_ref[rs, ksls[j]], sb_ref[i, ksls[j], :]))
    for j in range(GLA_HEADS):
        yg_ref[:, vsls[j]] = (_rms(o_ref[:, vsls[j]], glanorm_ref[...])
                              * _silu(r_ref[:, vsls[j]])).astype(BF16)

    merged = merged_ssd + jax.nn.sigmoid(gg_ref[...]) * _dot(yg_ref[...], wug_ref[...])
    h_ref[...] = x_ref[...] + _dot(merged.astype(BF16), wo_ref[...])


def _memkv_kernel(mem_ref, g_ref, w_ref, k_ref, v_ref):
    m = _rms(mem_ref[...], g_ref[...]).astype(BF16)
    kv = _dot(m, w_ref[...])
    k_ref[...] = kv[:, :D_MODEL].astype(BF16)
    v_ref[...] = kv[:, D_MODEL:].astype(BF16)


def _tail_kernel(h_ref, gx_ref, wq_ref, k_ref, v_ref, wxo_ref, gf_ref, wfi_ref, wfo_ref, gfin_ref,
                 out_ref, o_ref, act_ref):
    h1 = h_ref[...]
    q = _dot(_rms(h1, gx_ref[...]).astype(BF16), wq_ref[...])
    for j in range(XATTN_HEADS):
        sl = slice(j * XATTN_HEAD_DIM, (j + 1) * XATTN_HEAD_DIM)
        sc = _dot_nt(q[:, sl].astype(BF16), k_ref[:, sl]) * (XATTN_HEAD_DIM ** -0.5)
        e = jnp.exp(sc - jnp.max(sc, axis=-1, keepdims=True))
        p = e / jnp.sum(e, axis=-1, keepdims=True)
        o_ref[:, sl] = _dot(p.astype(BF16), v_ref[:, sl]).astype(BF16)
    h2 = h1 + _dot(o_ref[...], wxo_ref[...])
    n3 = _rms(h2, gf_ref[...]).astype(BF16)
    ff_block = D_FF // 2
    for cb in range(D_FF // ff_block):
        gate = _dot(n3, wfi_ref[:, cb * ff_block:(cb + 1) * ff_block])
        up = _dot(n3, wfi_ref[:, D_FF + cb * ff_block:D_FF + (cb + 1) * ff_block])
        act_ref[:, cb * ff_block:(cb + 1) * ff_block] = (_silu(gate) * up).astype(BF16)
    h3 = h2 + _dot(act_ref[...], wfo_ref[...])
    out_ref[...] = _rms(h3, gfin_ref[...])


def _resident(shape):
    return pl.BlockSpec(shape, lambda b, s: (0,) * len(shape), pipeline_mode=pl.Buffered(1))


def _row(v, width=None):
    v = v.reshape(1, -1).astype(F32)
    if width is not None and v.shape[1] < width:
        v = jnp.pad(v, ((0, 0), (0, width - v.shape[1])))
    return v


@functools.partial(jax.jit, static_argnames=("tile",))
def _forward(x, mem, norm_mix, w_in, ssd_conv_w, ssd_conv_b, ssd_dt_bias, ssd_A_log, ssd_D, ssd_norm,
             gla_w_a2, gla_b_a, gla_norm, w_up_ssd, w_up_gla, w_o, norm_xattn, norm_mem, w_xq, w_xkv,
             w_xo, norm_ffn, w_ffn_in, w_ffn_out, norm_final, *, tile):
    batch, seq, _ = x.shape
    mem_len = mem.shape[1]
    assert seq % tile == 0 and tile % CHUNK == 0
    grid = (batch, seq // tile)

    bounds = [0]
    for sz in IN_SIZES:
        bounds.append(bounds[-1] + sz)
    w = w_in[0]
    sec = lambda i: w[:, bounds[i]:bounds[i + 1]]
    pad = jnp.zeros((D_MODEL, LANES - SSD_HEADS - GLA_GATE_RANK), w.dtype)
    w_in_r = jnp.concatenate([sec(0), sec(1), sec(3), sec(4), sec(5), sec(6), sec(8), sec(9),
                              sec(2), sec(7), pad], axis=1).astype(BF16)
    wa2 = jnp.zeros((LANES, GLA_DK), F32).at[SMALL_A1:SMALL_A1 + GLA_GATE_RANK].set(gla_w_a2[0]).astype(BF16)
    dskip = jnp.broadcast_to(ssd_D[0].astype(F32)[:, None], (SSD_HEADS, tile))

    tok_spec = pl.BlockSpec((None, tile, D_MODEL), lambda b, s: (b, s, 0))

    mixer_inputs = [
        (x, tok_spec),
        (_row(norm_mix[0]), _resident((1, D_MODEL))),
        (w_in_r, _resident((D_MODEL, IN_WIDTH_PADDED))),
        (ssd_conv_w[0].reshape(SSD_CONV, SSD_CONV_CH), _resident((SSD_CONV, SSD_CONV_CH))),
        (_row(ssd_conv_b[0]), _resident((1, SSD_CONV_CH))),
        (_row(ssd_dt_bias[0], LANES), _resident((1, LANES))),
        (_row(ssd_A_log[0], LANES), _resident((1, LANES))),
        (dskip, _resident((SSD_HEADS, tile))),
        (_row(ssd_norm[0]), _resident((1, SSD_D_INNER))),
        (wa2, _resident((LANES, GLA_DK))),
        (_row(gla_b_a[0]), _resident((1, GLA_DK))),
        (_row(gla_norm[0]), _resident((1, GLA_HEAD_V))),
        (w_up_ssd[0].astype(BF16), _resident((SSD_D_INNER, D_MODEL))),
        (w_up_gla[0].astype(BF16), _resident((GLA_DV, D_MODEL))),
        (w_o[0].astype(BF16), _resident((D_MODEL, D_MODEL))),
    ]
    mixer_scratch = [
        pltpu.VMEM((tile, D_MODEL), BF16),
        pltpu.VMEM((tile, SSD_D_INNER), F32),
        pltpu.VMEM((tile + CONV_HALO, SSD_CONV_CH), F32),
        pltpu.VMEM((tile, GLA_DK), F32),
        pltpu.VMEM((tile, GLA_DK), F32),
        pltpu.VMEM((tile, GLA_DV), BF16),
        pltpu.VMEM((tile, GLA_DV), F32),
        pltpu.VMEM((tile, D_MODEL), F32),
        pltpu.VMEM((tile, D_MODEL), F32),
        pltpu.VMEM((tile, SSD_D_INNER), F32),
        pltpu.VMEM((tile, SSD_GROUPS * SSD_STATE), BF16),
        pltpu.VMEM((tile, SSD_GROUPS * SSD_STATE), BF16),
        pltpu.VMEM((tile, LANES), F32),
        pltpu.VMEM((tile, LANES), F32),
        pltpu.VMEM((tile, GLA_DK), F32),
        pltpu.VMEM((SSD_D_INNER, tile), F32),
        pltpu.VMEM((SSD_HEADS_PER_GROUP * SSD_HEAD_DIM, tile), BF16),
        pltpu.VMEM((tile, SSD_D_INNER), BF16),
        pltpu.VMEM((tile, GLA_DK), BF16),
        pltpu.VMEM((tile, GLA_DK), BF16),
        pltpu.VMEM((tile, GLA_DK), BF16),
        pltpu.VMEM((tile // CHUNK, GLA_DK, GLA_HEAD_V), BF16),
        pltpu.VMEM((tile, GLA_DV), F32),
        pltpu.VMEM((tile, GLA_DV), BF16),
        pltpu.VMEM((SSD_D_INNER, SSD_STATE), F32),
        pltpu.VMEM((GLA_DK, GLA_HEAD_V), F32),
    ]
    h1 = pl.pallas_call(
        functools.partial(_mixer_kernel, tile=tile),
        grid=grid,
        in_specs=[spec for _, spec in mixer_inputs],
        out_specs=tok_spec,
        out_shape=jax.ShapeDtypeStruct(x.shape, F32),
        scratch_shapes=mixer_scratch,
        compiler_params=pltpu.CompilerParams(
            dimension_semantics=("arbitrary", "arbitrary"), vmem_limit_bytes=VMEM_LIMIT_BYTES),
        name="mixer",
    )(*[a for a, _ in mixer_inputs])

    kv_shape = jax.ShapeDtypeStruct((batch, mem_len, D_MODEL), BF16)
    mem_spec = pl.BlockSpec((None, mem_len, D_MODEL), lambda b: (b, 0, 0))
    k_mem, v_mem = pl.pallas_call(
        _memkv_kernel,
        grid=(batch,),
        in_specs=[mem_spec,
                  pl.BlockSpec((1, D_MODEL), lambda b: (0, 0)),
                  pl.BlockSpec((D_MODEL, 2 * D_MODEL), lambda b: (0, 0))],
        out_specs=[mem_spec, mem_spec],
        out_shape=[kv_shape, kv_shape],
        compiler_params=pltpu.CompilerParams(
            dimension_semantics=("arbitrary",), vmem_limit_bytes=VMEM_LIMIT_BYTES),
        name="memkv",
    )(mem, _row(norm_mem[0]), w_xkv[0].astype(BF16))

    kvb_spec = pl.BlockSpec((None, mem_len, D_MODEL), lambda b, s: (b, 0, 0))
    out = pl.pallas_call(
        _tail_kernel,
        grid=grid,
        in_specs=[tok_spec,
                  _resident((1, D_MODEL)),
                  _resident((D_MODEL, D_MODEL)),
                  kvb_spec, kvb_spec,
                  _resident((D_MODEL, D_MODEL)),
                  _resident((1, D_MODEL)),
                  _resident((D_MODEL, 2 * D_FF)),
                  _resident((D_FF, D_MODEL)),
                  _resident((1, D_MODEL))],
        out_specs=tok_spec,
        out_shape=jax.ShapeDtypeStruct(x.shape, F32),
        scratch_shapes=[pltpu.VMEM((tile, D_MODEL), BF16),
                        pltpu.VMEM((tile, D_FF), BF16)],
        compiler_params=pltpu.CompilerParams(
            dimension_semantics=("arbitrary", "arbitrary"), vmem_limit_bytes=VMEM_LIMIT_BYTES),
        name="tail",
    )(h1, _row(norm_xattn[0]), w_xq[0].astype(BF16), k_mem, v_mem, w_xo[0].astype(BF16),
      _row(norm_ffn[0]), w_ffn_in[0].astype(BF16), w_ffn_out[0].astype(BF16), _row(norm_final))
    return out


def kernel(x, mem, norm_mix, w_in, ssd_conv_w, ssd_conv_b, ssd_dt_bias, ssd_A_log, ssd_D, ssd_norm,
           gla_w_a2, gla_b_a, gla_norm, w_up_ssd, w_up_gla, w_o, norm_xattn, norm_mem, w_xq, w_xkv,
           w_xo, norm_ffn, w_ffn_in, w_ffn_out, norm_final):
    return _forward(x, mem, norm_mix, w_in, ssd_conv_w, ssd_conv_b, ssd_dt_bias, ssd_A_log, ssd_D,
                    ssd_norm, gla_w_a2, gla_b_a, gla_norm, w_up_ssd, w_up_gla, w_o, norm_xattn,
                    norm_mem, w_xq, w_xkv, w_xo, norm_ffn, w_ffn_in, w_ffn_out, norm_final, tile=256)
```

```python
import functools

import jax
import jax.numpy as jnp
from jax import lax
from jax.experimental import pallas as pl
from jax.experimental.pallas import tpu as pltpu

F32 = jnp.float32
BF16 = jnp.bfloat16

D_MODEL = 1024
EPS = 1e-6
CHUNK = 64
SSD_D_INNER = 1024
SSD_HEAD_DIM = 64
SSD_HEADS = 16
SSD_GROUPS = 2
SSD_HEADS_PER_GROUP = SSD_HEADS // SSD_GROUPS
SSD_STATE = 128
SSD_CONV = 4
SSD_CONV_CH = SSD_D_INNER + 2 * SSD_GROUPS * SSD_STATE
GLA_HEADS = 4
GLA_DK = 512
GLA_DV = 1024
GLA_HEAD_K = GLA_DK // GLA_HEADS
GLA_HEAD_V = GLA_DV // GLA_HEADS
GLA_GATE_RANK = 16
GLA_TAU = 16.0
XATTN_HEADS = 4
XATTN_HEAD_DIM = D_MODEL // XATTN_HEADS
D_FF = 2816
IN_SIZES = (SSD_D_INNER, SSD_CONV_CH, SSD_HEADS, GLA_DK, GLA_DK, GLA_DV, GLA_DV, GLA_GATE_RANK,
            D_MODEL, D_MODEL)

LANES = 128
SUBLANES = 8
VMEM_LIMIT_BYTES = 56 * 1024 * 1024
MIXER_TILE = 256
TAIL_TILE = 512
PREP_STEPS = 16

OFF_Z = 0
OFF_XBC = OFF_Z + SSD_D_INNER
OFF_Q = OFF_XBC + SSD_CONV_CH
OFF_K = OFF_Q + GLA_DK
OFF_V = OFF_K + GLA_DK
OFF_R = OFF_V + GLA_DV
OFF_GS = OFF_R + GLA_DV
OFF_GG = OFF_GS + D_MODEL
OFF_SMALL = OFF_GG + D_MODEL
IN_WIDTH_PADDED = OFF_SMALL + LANES
SMALL_DT = 0
SMALL_A1 = SSD_HEADS

CONV_HALO = SUBLANES


def _dot(a, b):
    return jnp.dot(a, b, preferred_element_type=F32)


def _dot_nt(a, b):
    return lax.dot_general(a, b, (((1,), (1,)), ((), ())), preferred_element_type=F32)


def _dot_tn(a, b):
    return lax.dot_general(a, b, (((0,), (0,)), ((), ())), preferred_element_type=F32)


def _rms(x, g):
    return x * lax.rsqrt(jnp.mean(x * x, axis=-1, keepdims=True) + EPS) * g


def _silu(x):
    return x * jax.nn.sigmoid(x)


def _split3(x):
    hi = x.astype(BF16)
    r1 = x - hi.astype(F32)
    mid = r1.astype(BF16)
    lo = (r1 - mid.astype(F32)).astype(BF16)
    return hi, mid, lo


def _cumsum_rows(tri, x):
    hi, mid, lo = _split3(x)
    return _dot(tri, hi) + _dot(tri, mid) + _dot(tri, lo)


def _cumsum_lanes(x, upper):
    hi, mid, lo = _split3(x)
    return _dot(hi, upper) + _dot(mid, upper) + _dot(lo, upper)


def _mixer_kernel(x_ref, gmix_ref, win_ref, convw_ref, convb_ref, dtb_ref, alog_ref, dskip_ref,
                  ssdnorm_ref, wa2_ref, ba_ref, glanorm_ref, wus_ref, wug_ref, wo_ref,
                  h_ref,
                  n_ref, z_ref, xpad_ref, q_ref, k_ref, v_ref, r_ref, gs_ref, gg_ref,
                  xs_ref, b_ref, c_ref, dt_ref, a_ref, la_ref, yt_ref, xd_ref, ys_ref,
                  qt_ref, kt_ref, kh_ref, att_ref, sb_ref, o_ref, yg_ref,
                  sstate_ref, gstate_ref, *, tile):
    s = pl.program_id(1)

    @pl.when(s == 0)
    def _():
        sstate_ref[...] = jnp.zeros_like(sstate_ref)
        gstate_ref[...] = jnp.zeros_like(gstate_ref)
        xpad_ref[0:CONV_HALO, :] = jnp.zeros((CONV_HALO, SSD_CONV_CH), F32)

    n_ref[...] = _rms(x_ref[...], gmix_ref[...]).astype(BF16)

    def proj(off, width):
        return _dot(n_ref[...], win_ref[:, off:off + width])

    slab = 512
    deferred = [(dst, off, c0)
                for dst, off, width in ((q_ref, OFF_Q, GLA_DK), (k_ref, OFF_K, GLA_DK),
                                        (v_ref, OFF_V, GLA_DV), (r_ref, OFF_R, GLA_DV),
                                        (z_ref, OFF_Z, SSD_D_INNER), (gs_ref, OFF_GS, D_MODEL),
                                        (gg_ref, OFF_GG, D_MODEL))
                for c0 in range(0, width, slab)]

    def emit_proj(count=1):
        for _ in range(count):
            if deferred:
                dst, off, c0 = deferred.pop(0)
                dst[:, c0:c0 + slab] = proj(off + c0, slab).astype(dst.dtype)

    xpad_ref[CONV_HALO:CONV_HALO + tile, :] = proj(OFF_XBC, SSD_CONV_CH)
    small = proj(OFF_SMALL, LANES)

    dt = jax.nn.softplus(small + dtb_ref[...])
    dt_ref[...] = dt
    a_ref[...] = dt * (-jnp.exp(alog_ref[...]))
    logits = _dot(small.astype(BF16), wa2_ref[...]) + ba_ref[...]
    la_ref[...] = jax.nn.log_sigmoid(logits) / GLA_TAU

    col_block = 512
    for cb in range(SSD_CONV_CH // col_block):
        cols = slice(cb * col_block, (cb + 1) * col_block)
        acc = convb_ref[:, cols]
        for j in range(SSD_CONV):
            lo = CONV_HALO - (SSD_CONV - 1) + j
            acc = acc + convw_ref[j:j + 1, cols] * xpad_ref[lo:lo + tile, cols]
        act = _silu(acc)
        if cb < SSD_D_INNER // col_block:
            xs_ref[:, cols] = act
        else:
            bc = act.astype(BF16)
            half = col_block // 2
            b_ref[...] = bc[:, :half]
            c_ref[...] = bc[:, half:]
        emit_proj()
    xpad_ref[0:CONV_HALO, :] = xpad_ref[tile:tile + CONV_HALO, :]

    rid = lax.broadcasted_iota(jnp.int32, (tile, tile), 0)
    cid = lax.broadcasted_iota(jnp.int32, (tile, tile), 1)
    upper = rid <= cid
    a_t = a_ref[...].T
    acs_t = _cumsum_lanes(a_t, upper.astype(BF16))
    acs = acs_t.T
    dt_t = dt_ref[...].T
    xs_t = xs_ref[...].T
    hp = SSD_HEADS_PER_GROUP * SSD_HEAD_DIM
    for g in range(SSD_GROUPS):
        nsl = slice(g * SSD_STATE, (g + 1) * SSD_STATE)
        bg = b_ref[:, nsl]
        cg = c_ref[:, nsl]
        cb_t = jnp.where(upper, _dot_nt(bg, cg), 0.0)
        state = sstate_ref[g * hp:(g + 1) * hp, :]
        y_off_t = _dot_nt(state.astype(BF16), cg)
        for r in range(SSD_HEADS_PER_GROUP):
            h = g * SSD_HEADS_PER_GROUP + r
            psl = slice(h * SSD_HEAD_DIM, (h + 1) * SSD_HEAD_DIM)
            rsl = slice(r * SSD_HEAD_DIM, (r + 1) * SSD_HEAD_DIM)
            col = acs[:, h:h + 1]
            row = acs_t[h:h + 1, :]
            last = row[:, tile - 1:tile]
            m_t = (jnp.exp(jnp.minimum(row - col, 0.0)) * cb_t).astype(BF16)
            xh_t = xs_t[psl, :]
            xdt_t = xh_t * dt_t[h:h + 1, :]
            y_diag_t = _dot(xdt_t.astype(BF16), m_t)
            yt_ref[psl, :] = (y_diag_t + y_off_t[rsl, :] * jnp.exp(row)
                              + dskip_ref[h:h + 1, :] * xh_t)
            xd_ref[rsl, :] = (xdt_t * jnp.exp(last - row)).astype(BF16)
            sstate_ref[psl, :] = state[rsl, :] * jnp.exp(last)
            emit_proj()
        sstate_ref[g * hp:(g + 1) * hp, :] += _dot(xd_ref[...], bg)
    emit_proj(len(deferred))
    yz = yt_ref[...].T * _silu(z_ref[...])
    gn = SSD_D_INNER // SSD_GROUPS
    for g in range(SSD_GROUPS):
        csl = slice(g * gn, (g + 1) * gn)
        ys_ref[:, csl] = _rms(yz[:, csl], ssdnorm_ref[:, csl]).astype(BF16)
    merged_ssd = jax.nn.sigmoid(gs_ref[...]) * _dot(ys_ref[...], wus_ref[...])

    n_chunks = tile // CHUNK
    chunk_of = lambda idx: lax.shift_right_logical(idx, CHUNK.bit_length() - 1)
    tri_blocks = ((rid >= cid) & (chunk_of(rid) == chunk_of(cid))).astype(BF16)
    bcum = _cumsum_rows(tri_blocks, la_ref[...])
    blast = jnp.concatenate(
        [jnp.broadcast_to(bcum[(i + 1) * CHUNK - 1:(i + 1) * CHUNK, :], (CHUNK, GLA_DK))
         for i in range(n_chunks)], axis=0)
    kk = k_ref[...]
    qt_ref[...] = (q_ref[...] * (GLA_HEAD_K ** -0.5) * jnp.exp(bcum)).astype(BF16)
    kt_ref[...] = (kk * jnp.exp(-bcum)).astype(BF16)
    kh_ref[...] = (kk * jnp.exp(blast - bcum)).astype(BF16)
    dec_t = jnp.exp(blast).T
    causal = (lax.broadcasted_iota(jnp.int32, (CHUNK, CHUNK), 0)
              >= lax.broadcasted_iota(jnp.int32, (CHUNK, CHUNK), 1))
    ksls = [slice(j * GLA_HEAD_K, (j + 1) * GLA_HEAD_K) for j in range(GLA_HEADS)]
    vsls = [slice(j * GLA_HEAD_V, (j + 1) * GLA_HEAD_V) for j in range(GLA_HEADS)]
    for i in range(n_chunks):
        rs = slice(i * CHUNK, (i + 1) * CHUNK)
        for j in range(GLA_HEADS):
            scores = _dot_nt(qt_ref[rs, ksls[j]], kt_ref[rs, ksls[j]])
            att_ref[i * GLA_HEADS + j] = jnp.where(causal, scores, 0.0).astype(BF16)
    states = [gstate_ref[ksl, :] for ksl in ksls]
    for i in range(n_chunks):
        rs = slice(i * CHUNK, (i + 1) * CHUNK)
        for j in range(GLA_HEADS):
            sb_ref[i, ksls[j], :] = states[j].astype(BF16)
            update = _dot_tn(kh_ref[rs, ksls[j]], v_ref[rs, vsls[j]])
            states[j] = states[j] * dec_t[ksls[j], i * CHUNK:i * CHUNK + 1] + update
    for j in range(GLA_HEADS):
        gstate_ref[ksls[j], :] = states[j]
    for i in range(n_chunks):
        rs = slice(i * CHUNK, (i + 1) * CHUNK)
        for j in range(GLA_HEADS):
            o_ref[rs, vsls[j]] = (_dot(att_ref[i * GLA_HEADS + j], v_ref[rs, vsls[j]])
                                  + _dot(qt_ref[rs, ksls[j]], sb_ref[i, ksls[j], :]))
    for j in range(GLA_HEADS):
        yg_ref[:, vsls[j]] = (_rms(o_ref[:, vsls[j]], glanorm_ref[...])
                              * _silu(r_ref[:, vsls[j]])).astype(BF16)

    merged = merged_ssd + jax.nn.sigmoid(gg_ref[...]) * _dot(yg_ref[...], wug_ref[...])
    h_ref[...] = x_ref[...] + _dot(merged.astype(BF16), wo_ref[...])


def _memkv_kernel(mem_ref, g_ref, w_ref, k_ref, v_ref):
    m = _rms(mem_ref[...], g_ref[...]).astype(BF16)
    kv = _dot(m, w_ref[...])
    k_ref[...] = kv[:, :D_MODEL].astype(BF16)
    v_ref[...] = kv[:, D_MODEL:].astype(BF16)


def _tail_kernel(h_ref, gx_ref, wq_ref, k_ref, v_ref, wxo_ref, gf_ref, wfi_ref, wfo_ref, gfin_ref,
                 out_ref, o_ref, act_ref):
    h1 = h_ref[...]
    q = _dot(_rms(h1, gx_ref[...]).astype(BF16), wq_ref[...])
    for j in range(XATTN_HEADS):
        sl = slice(j * XATTN_HEAD_DIM, (j + 1) * XATTN_HEAD_DIM)
        sc = _dot_nt(q[:, sl].astype(BF16), k_ref[:, sl]) * (XATTN_HEAD_DIM ** -0.5)
        e = jnp.exp(sc - jnp.max(sc, axis=-1, keepdims=True))
        p = e / jnp.sum(e, axis=-1, keepdims=True)
        o_ref[:, sl] = _dot(p.astype(BF16), v_ref[:, sl]).astype(BF16)
    h2 = h1 + _dot(o_ref[...], wxo_ref[...])
    n3 = _rms(h2, gf_ref[...]).astype(BF16)
    ff_block = D_FF // 2
    for cb in range(D_FF // ff_block):
        gate = _dot(n3, wfi_ref[:, cb * ff_block:(cb + 1) * ff_block])
        up = _dot(n3, wfi_ref[:, D_FF + cb * ff_block:D_FF + (cb + 1) * ff_block])
        act_ref[:, cb * ff_block:(cb + 1) * ff_block] = (_silu(gate) * up).astype(BF16)
    h3 = h2 + _dot(act_ref[...], wfo_ref[...])
    out_ref[...] = _rms(h3, gfin_ref[...])


def _in_proj_segments():
    bounds = [0]
    for sz in IN_SIZES:
        bounds.append(bounds[-1] + sz)
    return ((OFF_Z, bounds[0], bounds[2] - bounds[0]),
            (OFF_Q, bounds[3], bounds[7] - bounds[3]),
            (OFF_GS, bounds[8], bounds[10] - bounds[8]),
            (OFF_SMALL + SMALL_DT, bounds[2], SSD_HEADS),
            (OFF_SMALL + SMALL_A1, bounds[7], GLA_GATE_RANK))


def _prep_kernel(win_ref, *refs):
    n_plain = (len(refs) - 1) // 2
    plain_in, owin_ref, plain_out = refs[:n_plain], refs[n_plain], refs[n_plain + 1:]
    for dst, src, width in _in_proj_segments():
        owin_ref[:, dst:dst + width] = win_ref[:, src:src + width].astype(BF16)
    used = OFF_SMALL + SMALL_A1 + GLA_GATE_RANK
    owin_ref[:, used:] = jnp.zeros((owin_ref.shape[0], IN_WIDTH_PADDED - used), BF16)
    for src_ref, dst_ref in zip(plain_in, plain_out):
        dst_ref[...] = src_ref[...].astype(BF16)


def _resident(shape):
    return pl.BlockSpec(shape, lambda b, s: (0,) * len(shape), pipeline_mode=pl.Buffered(1))


def _row(v, width=None):
    v = v.reshape(1, -1).astype(F32)
    if width is not None and v.shape[1] < width:
        v = jnp.pad(v, ((0, 0), (0, width - v.shape[1])))
    return v


@functools.partial(jax.jit, static_argnames=("tile", "tail_tile"))
def _forward(x, mem, norm_mix, w_in, ssd_conv_w, ssd_conv_b, ssd_dt_bias, ssd_A_log, ssd_D, ssd_norm,
             gla_w_a2, gla_b_a, gla_norm, w_up_ssd, w_up_gla, w_o, norm_xattn, norm_mem, w_xq, w_xkv,
             w_xo, norm_ffn, w_ffn_in, w_ffn_out, norm_final, *, tile, tail_tile):
    batch, seq, _ = x.shape
    mem_len = mem.shape[1]
    assert seq % tile == 0 and tile % CHUNK == 0
    grid = (batch, seq // tile)

    plain = [w_up_ssd[0], w_up_gla[0], w_o[0], w_xq[0], w_xkv[0], w_xo[0], w_ffn_in[0], w_ffn_out[0]]
    row_block = lambda a: pl.BlockSpec((a.shape[0] // PREP_STEPS, a.shape[1]), lambda i: (i, 0))
    win_out = jax.ShapeDtypeStruct((D_MODEL, IN_WIDTH_PADDED), BF16)
    (w_in_r, w_up_ssd_b, w_up_gla_b, w_o_b, w_xq_b, w_xkv_b, w_xo_b, w_ffn_in_b,
     w_ffn_out_b) = pl.pallas_call(
        _prep_kernel,
        grid=(PREP_STEPS,),
        in_specs=[row_block(w_in[0])] + [row_block(a) for a in plain],
        out_specs=[row_block(win_out)] + [row_block(a) for a in plain],
        out_shape=[win_out] + [jax.ShapeDtypeStruct(a.shape, BF16) for a in plain],
        compiler_params=pltpu.CompilerParams(
            dimension_semantics=("arbitrary",), vmem_limit_bytes=VMEM_LIMIT_BYTES),
        name="prep",
    )(w_in[0], *plain)
    wa2 = jnp.zeros((LANES, GLA_DK), F32).at[SMALL_A1:SMALL_A1 + GLA_GATE_RANK].set(gla_w_a2[0]).astype(BF16)
    dskip = jnp.broadcast_to(ssd_D[0].astype(F32)[:, None], (SSD_HEADS, tile))

    tok_spec = pl.BlockSpec((None, tile, D_MODEL), lambda b, s: (b, s, 0))

    mixer_inputs = [
        (x, tok_spec),
        (_row(norm_mix[0]), _resident((1, D_MODEL))),
        (w_in_r, _resident((D_MODEL, IN_WIDTH_PADDED))),
        (ssd_conv_w[0].reshape(SSD_CONV, SSD_CONV_CH), _resident((SSD_CONV, SSD_CONV_CH))),
        (_row(ssd_conv_b[0]), _resident((1, SSD_CONV_CH))),
        (_row(ssd_dt_bias[0], LANES), _resident((1, LANES))),
        (_row(ssd_A_log[0], LANES), _resident((1, LANES))),
        (dskip, _resident((SSD_HEADS, tile))),
        (_row(ssd_norm[0]), _resident((1, SSD_D_INNER))),
        (wa2, _resident((LANES, GLA_DK))),
        (_row(gla_b_a[0]), _resident((1, GLA_DK))),
        (_row(gla_norm[0]), _resident((1, GLA_HEAD_V))),
        (w_up_ssd_b, _resident((SSD_D_INNER, D_MODEL))),
        (w_up_gla_b, _resident((GLA_DV, D_MODEL))),
        (w_o_b, _resident((D_MODEL, D_MODEL))),
    ]
    mixer_scratch = [
        pltpu.VMEM((tile, D_MODEL), BF16),
        pltpu.VMEM((tile, SSD_D_INNER), F32),
        pltpu.VMEM((tile + CONV_HALO, SSD_CONV_CH), F32),
        pltpu.VMEM((tile, GLA_DK), F32),
        pltpu.VMEM((tile, GLA_DK), F32),
        pltpu.VMEM((tile, GLA_DV), BF16),
        pltpu.VMEM((tile, GLA_DV), F32),
        pltpu.VMEM((tile, D_MODEL), F32),
        pltpu.VMEM((tile, D_MODEL), F32),
        pltpu.VMEM((tile, SSD_D_INNER), F32),
        pltpu.VMEM((tile, SSD_GROUPS * SSD_STATE), BF16),
        pltpu.VMEM((tile, SSD_GROUPS * SSD_STATE), BF16),
        pltpu.VMEM((tile, LANES), F32),
        pltpu.VMEM((tile, LANES), F32),
        pltpu.VMEM((tile, GLA_DK), F32),
        pltpu.VMEM((SSD_D_INNER, tile), F32),
        pltpu.VMEM((SSD_HEADS_PER_GROUP * SSD_HEAD_DIM, tile), BF16),
        pltpu.VMEM((tile, SSD_D_INNER), BF16),
        pltpu.VMEM((tile, GLA_DK), BF16),
        pltpu.VMEM((tile, GLA_DK), BF16),
        pltpu.VMEM((tile, GLA_DK), BF16),
        pltpu.VMEM((tile // CHUNK * GLA_HEADS, CHUNK, CHUNK), BF16),
        pltpu.VMEM((tile // CHUNK, GLA_DK, GLA_HEAD_V), BF16),
        pltpu.VMEM((tile, GLA_DV), F32),
        pltpu.VMEM((tile, GLA_DV), BF16),
        pltpu.VMEM((SSD_D_INNER, SSD_STATE), F32),
        pltpu.VMEM((GLA_DK, GLA_HEAD_V), F32),
    ]
    h1 = pl.pallas_call(
        functools.partial(_mixer_kernel, tile=tile),
        grid=grid,
        in_specs=[spec for _, spec in mixer_inputs],
        out_specs=tok_spec,
        out_shape=jax.ShapeDtypeStruct(x.shape, F32),
        scratch_shapes=mixer_scratch,
        compiler_params=pltpu.CompilerParams(
            dimension_semantics=("arbitrary", "arbitrary"), vmem_limit_bytes=VMEM_LIMIT_BYTES),
        name="mixer",
    )(*[a for a, _ in mixer_inputs])

    kv_shape = jax.ShapeDtypeStruct((batch, mem_len, D_MODEL), BF16)
    mem_spec = pl.BlockSpec((None, mem_len, D_MODEL), lambda b: (b, 0, 0))
    k_mem, v_mem = pl.pallas_call(
        _memkv_kernel,
        grid=(batch,),
        in_specs=[mem_spec,
                  pl.BlockSpec((1, D_MODEL), lambda b: (0, 0)),
                  pl.BlockSpec((D_MODEL, 2 * D_MODEL), lambda b: (0, 0))],
        out_specs=[mem_spec, mem_spec],
        out_shape=[kv_shape, kv_shape],
        compiler_params=pltpu.CompilerParams(
            dimension_semantics=("arbitrary",), vmem_limit_bytes=VMEM_LIMIT_BYTES),
        name="memkv",
    )(mem, _row(norm_mem[0]), w_xkv_b)

    kvb_spec = pl.BlockSpec((None, mem_len, D_MODEL), lambda b, s: (b, 0, 0))
    assert seq % tail_tile == 0
    tail_spec = pl.BlockSpec((None, tail_tile, D_MODEL), lambda b, s: (b, s, 0))
    out = pl.pallas_call(
        _tail_kernel,
        grid=(batch, seq // tail_tile),
        in_specs=[tail_spec,
                  _resident((1, D_MODEL)),
                  _resident((D_MODEL, D_MODEL)),
                  kvb_spec, kvb_spec,
                  _resident((D_MODEL, D_MODEL)),
                  _resident((1, D_MODEL)),
                  _resident((D_MODEL, 2 * D_FF)),
                  _resident((D_FF, D_MODEL)),
                  _resident((1, D_MODEL))],
        out_specs=tail_spec,
        out_shape=jax.ShapeDtypeStruct(x.shape, F32),
        scratch_shapes=[pltpu.VMEM((tail_tile, D_MODEL), BF16),
                        pltpu.VMEM((tail_tile, D_FF), BF16)],
        compiler_params=pltpu.CompilerParams(
            dimension_semantics=("arbitrary", "arbitrary"), vmem_limit_bytes=VMEM_LIMIT_BYTES),
        name="tail",
    )(h1, _row(norm_xattn[0]), w_xq_b, k_mem, v_mem, w_xo_b,
      _row(norm_ffn[0]), w_ffn_in_b, w_ffn_out_b, _row(norm_final))
    return out


def kernel(x, mem, norm_mix, w_in, ssd_conv_w, ssd_conv_b, ssd_dt_bias, ssd_A_log, ssd_D, ssd_norm,
           gla_w_a2, gla_b_a, gla_norm, w_up_ssd, w_up_gla, w_o, norm_xattn, norm_mem, w_xq, w_xkv,
           w_xo, norm_ffn, w_ffn_in, w_ffn_out, norm_final):
    return _forward(x, mem, norm_mix, w_in, ssd_conv_w, ssd_conv_b, ssd_dt_bias, ssd_A_log, ssd_D,
                    ssd_norm, gla_w_a2, gla_b_a, gla_norm, w_up_ssd, w_up_gla, w_o, norm_xattn,
                    norm_mem, w_xq, w_xkv, w_xo, norm_ffn, w_ffn_in, w_ffn_out, norm_final,
                    tile=MIXER_TILE, tail_tile=TAIL_TILE)
```

```python
import functools

import jax
import jax.numpy as jnp
from jax import lax
from jax.experimental import pallas as pl
from jax.experimental.pallas import tpu as pltpu

F32 = jnp.float32
BF16 = jnp.bfloat16

D_MODEL = 1024
EPS = 1e-6
CHUNK = 64
SSD_D_INNER = 1024
SSD_HEAD_DIM = 64
SSD_HEADS = 16
SSD_GROUPS = 2
SSD_HEADS_PER_GROUP = SSD_HEADS // SSD_GROUPS
SSD_STATE = 128
SSD_CONV = 4
SSD_CONV_CH = SSD_D_INNER + 2 * SSD_GROUPS * SSD_STATE
GLA_HEADS = 4
GLA_DK = 512
GLA_DV = 1024
GLA_HEAD_K = GLA_DK // GLA_HEADS
GLA_HEAD_V = GLA_DV // GLA_HEADS
GLA_GATE_RANK = 16
GLA_TAU = 16.0
XATTN_HEADS = 4
XATTN_HEAD_DIM = D_MODEL // XATTN_HEADS
D_FF = 2816
IN_SIZES = (SSD_D_INNER, SSD_CONV_CH, SSD_HEADS, GLA_DK, GLA_DK, GLA_DV, GLA_DV, GLA_GATE_RANK,
            D_MODEL, D_MODEL)

LANES = 128
SUBLANES = 8
VMEM_LIMIT_BYTES = 56 * 1024 * 1024
MIXER_TILE = 256
TAIL_TILE = 512
PREP_STEPS = 16

OFF_Z = 0
OFF_XBC = OFF_Z + SSD_D_INNER
OFF_Q = OFF_XBC + SSD_CONV_CH
OFF_K = OFF_Q + GLA_DK
OFF_V = OFF_K + GLA_DK
OFF_R = OFF_V + GLA_DV
OFF_GS = OFF_R + GLA_DV
OFF_GG = OFF_GS + D_MODEL
OFF_SMALL = OFF_GG + D_MODEL
IN_WIDTH_PADDED = OFF_SMALL + LANES
SMALL_DT = 0
SMALL_A1 = SSD_HEADS

CONV_HALO = SUBLANES


def _dot(a, b):
    return jnp.dot(a, b, preferred_element_type=F32)


def _dot_nt(a, b):
    return lax.dot_general(a, b, (((1,), (1,)), ((), ())), preferred_element_type=F32)


def _dot_tn(a, b):
    return lax.dot_general(a, b, (((0,), (0,)), ((), ())), preferred_element_type=F32)


def _rms(x, g):
    return x * lax.rsqrt(jnp.mean(x * x, axis=-1, keepdims=True) + EPS) * g


def _silu(x):
    return x * jax.nn.sigmoid(x)


def _split3(x):
    hi = x.astype(BF16)
    r1 = x - hi.astype(F32)
    mid = r1.astype(BF16)
    lo = (r1 - mid.astype(F32)).astype(BF16)
    return hi, mid, lo


def _cumsum_rows(tri, x):
    hi, mid, lo = _split3(x)
    return _dot(tri, hi) + _dot(tri, mid) + _dot(tri, lo)


def _cumsum_lanes(x, upper):
    hi, mid, lo = _split3(x)
    return _dot(hi, upper) + _dot(mid, upper) + _dot(lo, upper)


def _order_after(n_ref, tok):
    half_word = jnp.uint32(16)
    bits = pltpu.bitcast(tok, jnp.uint32)
    zero = pltpu.bitcast(lax.shift_right_logical(lax.shift_right_logical(bits, half_word), half_word), F32)
    n_ref[0:16, 0:LANES] = (n_ref[0:16, 0:LANES].astype(F32) + zero).astype(BF16)


def _mixer_kernel(x_ref, gmix_ref, win_ref, convw_ref, convb_ref, dtb_ref, alog_ref, dskip_ref,
                  ssdnorm_ref, wa2_ref, ba_ref, glanorm_ref, wus_ref, wug_ref, wo_ref,
                  h_ref,
                  n_ref, z_ref, xpad_ref, q_ref, k_ref, v_ref, r_ref, gs_ref, gg_ref,
                  xs_ref, b_ref, c_ref, dt_ref, a_ref, la_ref, yt_ref, xd_ref, ys_ref,
                  qt_ref, kt_ref, kh_ref, att_ref, sb_ref, o_ref, yg_ref,
                  sstate_ref, gstate_ref, *, tile):
    s = pl.program_id(1)

    @pl.when(s == 0)
    def _():
        sstate_ref[...] = jnp.zeros_like(sstate_ref)
        gstate_ref[...] = jnp.zeros_like(gstate_ref)
        xpad_ref[0:CONV_HALO, :] = jnp.zeros((CONV_HALO, SSD_CONV_CH), F32)

    n_ref[...] = _rms(x_ref[...], gmix_ref[...]).astype(BF16)

    def proj(off, width):
        return _dot(n_ref[...], win_ref[:, off:off + width])

    slab = 512
    deferred = [(dst, off, c0)
                for dst, off, width in ((q_ref, OFF_Q, GLA_DK), (k_ref, OFF_K, GLA_DK),
                                        (v_ref, OFF_V, GLA_DV), (r_ref, OFF_R, GLA_DV),
                                        (z_ref, OFF_Z, SSD_D_INNER), (gs_ref, OFF_GS, D_MODEL),
                                        (gg_ref, OFF_GG, D_MODEL))
                for c0 in range(0, width, slab)]

    def emit_proj(count=1, token=None):
        for _ in range(count):
            if deferred:
                dst, off, c0 = deferred.pop(0)
                if token is not None:
                    _order_after(n_ref, token)
                dst[:, c0:c0 + slab] = proj(off + c0, slab).astype(dst.dtype)

    xpad_ref[CONV_HALO:CONV_HALO + tile, :] = proj(OFF_XBC, SSD_CONV_CH)
    small = proj(OFF_SMALL, LANES)

    dt = jax.nn.softplus(small + dtb_ref[...])
    dt_ref[...] = dt
    a_ref[...] = dt * (-jnp.exp(alog_ref[...]))
    logits = _dot(small.astype(BF16), wa2_ref[...]) + ba_ref[...]
    la_ref[...] = jax.nn.log_sigmoid(logits) / GLA_TAU

    col_block = SSD_GROUPS * SSD_STATE
    for cb in range(SSD_CONV_CH // col_block):
        cols = slice(cb * col_block, (cb + 1) * col_block)
        acc = convb_ref[:, cols]
        for j in range(SSD_CONV):
            lo = CONV_HALO - (SSD_CONV - 1) + j
            acc = acc + convw_ref[j:j + 1, cols] * xpad_ref[lo:lo + tile, cols]
        act = _silu(acc)
        if cb < SSD_D_INNER // col_block:
            xs_ref[:, cols] = act
        elif cb == SSD_D_INNER // col_block:
            b_ref[...] = act.astype(BF16)
        else:
            c_ref[...] = act.astype(BF16)
        emit_proj(token=act[0:16, 0:LANES])
    xpad_ref[0:CONV_HALO, :] = xpad_ref[tile:tile + CONV_HALO, :]

    rid = lax.broadcasted_iota(jnp.int32, (tile, tile), 0)
    cid = lax.broadcasted_iota(jnp.int32, (tile, tile), 1)
    upper = rid <= cid
    a_t = a_ref[...].T
    acs_t = _cumsum_lanes(a_t, upper.astype(BF16))
    acs = acs_t.T
    dt_t = dt_ref[...].T
    xs_t = xs_ref[...].T
    hp = SSD_HEADS_PER_GROUP * SSD_HEAD_DIM
    for g in range(SSD_GROUPS):
        nsl = slice(g * SSD_STATE, (g + 1) * SSD_STATE)
        bg = b_ref[:, nsl]
        cg = c_ref[:, nsl]
        cb_t = jnp.where(upper, _dot_nt(bg, cg), 0.0)
        state = sstate_ref[g * hp:(g + 1) * hp, :]
        y_off_t = _dot_nt(state.astype(BF16), cg)
        for r in range(SSD_HEADS_PER_GROUP):
            h = g * SSD_HEADS_PER_GROUP + r
            psl = slice(h * SSD_HEAD_DIM, (h + 1) * SSD_HEAD_DIM)
            rsl = slice(r * SSD_HEAD_DIM, (r + 1) * SSD_HEAD_DIM)
            col = acs[:, h:h + 1]
            row = acs_t[h:h + 1, :]
            last = row[:, tile - 1:tile]
            m_t = (jnp.exp(jnp.minimum(row - col, 0.0)) * cb_t).astype(BF16)
            xh_t = xs_t[psl, :]
            xdt_t = xh_t * dt_t[h:h + 1, :]
            y_diag_t = _dot(xdt_t.astype(BF16), m_t)
            yt_ref[psl, :] = (y_diag_t + y_off_t[rsl, :] * jnp.exp(row)
                              + dskip_ref[h:h + 1, :] * xh_t)
            xd_f = xdt_t * jnp.exp(last - row)
            xd_ref[rsl, :] = xd_f.astype(BF16)
            sstate_ref[psl, :] = state[rsl, :] * jnp.exp(last)
            if h in (1, 3, 6, 9, 12, 15):
                emit_proj(token=xd_f[0:16, 0:LANES])
        sstate_ref[g * hp:(g + 1) * hp, :] += _dot(xd_ref[...], bg)
    emit_proj(len(deferred))
    yz = yt_ref[...].T * _silu(z_ref[...])
    gn = SSD_D_INNER // SSD_GROUPS
    for g in range(SSD_GROUPS):
        csl = slice(g * gn, (g + 1) * gn)
        ys_ref[:, csl] = _rms(yz[:, csl], ssdnorm_ref[:, csl]).astype(BF16)
    merged_ssd = jax.nn.sigmoid(gs_ref[...]) * _dot(ys_ref[...], wus_ref[...])

    n_chunks = tile // CHUNK
    chunk_of = lambda idx: lax.shift_right_logical(idx, CHUNK.bit_length() - 1)
    tri_blocks = ((rid >= cid) & (chunk_of(rid) == chunk_of(cid))).astype(BF16)
    bcum = _cumsum_rows(tri_blocks, la_ref[...])
    blast = jnp.concatenate(
        [jnp.broadcast_to(bcum[(i + 1) * CHUNK - 1:(i + 1) * CHUNK, :], (CHUNK, GLA_DK))
         for i in range(n_chunks)], axis=0)
    kk = k_ref[...]
    qt_ref[...] = (q_ref[...] * (GLA_HEAD_K ** -0.5) * jnp.exp(bcum)).astype(BF16)
    kt_ref[...] = (kk * jnp.exp(-bcum)).astype(BF16)
    kh_ref[...] = (kk * jnp.exp(blast - bcum)).astype(BF16)
    dec_t = jnp.exp(blast).T
    causal = (lax.broadcasted_iota(jnp.int32, (CHUNK, CHUNK), 0)
              >= lax.broadcasted_iota(jnp.int32, (CHUNK, CHUNK), 1))
    ksls = [slice(j * GLA_HEAD_K, (j + 1) * GLA_HEAD_K) for j in range(GLA_HEADS)]
    vsls = [slice(j * GLA_HEAD_V, (j + 1) * GLA_HEAD_V) for j in range(GLA_HEADS)]
    for i in range(n_chunks):
        rs = slice(i * CHUNK, (i + 1) * CHUNK)
        for j in range(GLA_HEADS):
            scores = _dot_nt(qt_ref[rs, ksls[j]], kt_ref[rs, ksls[j]])
            att_ref[i * GLA_HEADS + j] = jnp.where(causal, scores, 0.0).astype(BF16)
    states = [gstate_ref[ksl, :] for ksl in ksls]
    for i in range(n_chunks):
        rs = slice(i * CHUNK, (i + 1) * CHUNK)
        for j in range(GLA_HEADS):
            sb_ref[i, ksls[j], :] = states[j].astype(BF16)
            update = _dot_tn(kh_ref[rs, ksls[j]], v_ref[rs, vsls[j]])
            states[j] = states[j] * dec_t[ksls[j], i * CHUNK:i * CHUNK + 1] + update
    for j in range(GLA_HEADS):
        gstate_ref[ksls[j], :] = states[j]
    for i in range(n_chunks):
        rs = slice(i * CHUNK, (i + 1) * CHUNK)
        for j in range(GLA_HEADS):
            o_ref[rs, vsls[j]] = (_dot(att_ref[i * GLA_HEADS + j], v_ref[rs, vsls[j]])
                                  + _dot(qt_ref[rs, ksls[j]], sb_ref[i, ksls[j], :]))
    for j in range(GLA_HEADS):
        yg_ref[:, vsls[j]] = (_rms(o_ref[:, vsls[j]], glanorm_ref[...])
                              * _silu(r_ref[:, vsls[j]])).astype(BF16)

    merged = merged_ssd + jax.nn.sigmoid(gg_ref[...]) * _dot(yg_ref[...], wug_ref[...])
    h_ref[...] = x_ref[...] + _dot(merged.astype(BF16), wo_ref[...])


def _memkv_kernel(mem_ref, g_ref, wkv_ref, wq_ref, wo_ref, wqk_ref, vwo_ref):
    mem_len = mem_ref.shape[0]
    m = _rms(mem_ref[...], g_ref[...]).astype(BF16)
    kv = _dot(m, wkv_ref[...])
    for j in range(XATTN_HEADS):
        dsl = slice(j * XATTN_HEAD_DIM, (j + 1) * XATTN_HEAD_DIM)
        msl = slice(j * mem_len, (j + 1) * mem_len)
        k_j = kv[:, dsl].astype(BF16)
        v_j = kv[:, D_MODEL + j * XATTN_HEAD_DIM:D_MODEL + (j + 1) * XATTN_HEAD_DIM].astype(BF16)
        wqk_ref[:, msl] = (_dot_nt(wq_ref[:, dsl], k_j) * (XATTN_HEAD_DIM ** -0.5)).astype(BF16)
        vwo_ref[msl, :] = _dot(v_j, wo_ref[dsl, :]).astype(BF16)


def _tail_kernel(h_ref, gx_ref, wqk_ref, vwo_ref, gf_ref, wfi_ref, wfo_ref, gfin_ref,
                 out_ref, p_ref, act_ref):
    h1 = h_ref[...]
    sc = _dot(_rms(h1, gx_ref[...]).astype(BF16), wqk_ref[...])
    mem_len = wqk_ref.shape[1] // XATTN_HEADS
    for j in range(XATTN_HEADS):
        msl = slice(j * mem_len, (j + 1) * mem_len)
        e = jnp.exp(sc[:, msl] - jnp.max(sc[:, msl], axis=-1, keepdims=True))
        p_ref[:, msl] = (e / jnp.sum(e, axis=-1, keepdims=True)).astype(BF16)
    h2 = h1 + _dot(p_ref[...], vwo_ref[...])
    n3 = _rms(h2, gf_ref[...]).astype(BF16)
    ff_block = D_FF // 2
    for cb in range(D_FF // ff_block):
        gate = _dot(n3, wfi_ref[:, cb * ff_block:(cb + 1) * ff_block])
        up = _dot(n3, wfi_ref[:, D_FF + cb * ff_block:D_FF + (cb + 1) * ff_block])
        act_ref[:, cb * ff_block:(cb + 1) * ff_block] = (_silu(gate) * up).astype(BF16)
    h3 = h2 + _dot(act_ref[...], wfo_ref[...])
    out_ref[...] = _rms(h3, gfin_ref[...])


def _in_proj_segments():
    bounds = [0]
    for sz in IN_SIZES:
        bounds.append(bounds[-1] + sz)
    return ((OFF_Z, bounds[0], bounds[2] - bounds[0]),
            (OFF_Q, bounds[3], bounds[7] - bounds[3]),
            (OFF_GS, bounds[8], bounds[10] - bounds[8]),
            (OFF_SMALL + SMALL_DT, bounds[2], SSD_HEADS),
            (OFF_SMALL + SMALL_A1, bounds[7], GLA_GATE_RANK))


def _prep_kernel(win_ref, *refs):
    n_plain = (len(refs) - 1) // 2
    plain_in, owin_ref, plain_out = refs[:n_plain], refs[n_plain], refs[n_plain + 1:]
    for dst, src, width in _in_proj_segments():
        owin_ref[:, dst:dst + width] = win_ref[:, src:src + width].astype(BF16)
    used = OFF_SMALL + SMALL_A1 + GLA_GATE_RANK
    owin_ref[:, used:] = jnp.zeros((owin_ref.shape[0], IN_WIDTH_PADDED - used), BF16)
    for src_ref, dst_ref in zip(plain_in, plain_out):
        dst_ref[...] = src_ref[...].astype(BF16)


def _resident(shape):
    return pl.BlockSpec(shape, lambda b, s: (0,) * len(shape), pipeline_mode=pl.Buffered(1))


def _row(v, width=None):
    v = v.reshape(1, -1).astype(F32)
    if width is not None and v.shape[1] < width:
        v = jnp.pad(v, ((0, 0), (0, width - v.shape[1])))
    return v


@functools.partial(jax.jit, static_argnames=("tile", "tail_tile"))
def _forward(x, mem, norm_mix, w_in, ssd_conv_w, ssd_conv_b, ssd_dt_bias, ssd_A_log, ssd_D, ssd_norm,
             gla_w_a2, gla_b_a, gla_norm, w_up_ssd, w_up_gla, w_o, norm_xattn, norm_mem, w_xq, w_xkv,
             w_xo, norm_ffn, w_ffn_in, w_ffn_out, norm_final, *, tile, tail_tile):
    batch, seq, _ = x.shape
    mem_len = mem.shape[1]
    assert seq % tile == 0 and tile % CHUNK == 0
    grid = (batch, seq // tile)

    plain = [w_up_ssd, w_up_gla, w_o, w_xq, w_xkv, w_xo, w_ffn_in, w_ffn_out]
    row_block_in = lambda a: pl.BlockSpec((None, a.shape[1] // PREP_STEPS, a.shape[2]),
                                          lambda i: (0, i, 0))
    row_block_out = lambda a: pl.BlockSpec((a.shape[0] // PREP_STEPS, a.shape[1]), lambda i: (i, 0))
    outs = ([jax.ShapeDtypeStruct((D_MODEL, IN_WIDTH_PADDED), BF16)]
            + [jax.ShapeDtypeStruct(a.shape[1:], BF16) for a in plain])
    (w_in_r, w_up_ssd_b, w_up_gla_b, w_o_b, w_xq_b, w_xkv_b, w_xo_b, w_ffn_in_b,
     w_ffn_out_b) = pl.pallas_call(
        _prep_kernel,
        grid=(PREP_STEPS,),
        in_specs=[row_block_in(a) for a in [w_in] + plain],
        out_specs=[row_block_out(a) for a in outs],
        out_shape=outs,
        compiler_params=pltpu.CompilerParams(
            dimension_semantics=("arbitrary",), vmem_limit_bytes=VMEM_LIMIT_BYTES),
        name="prep",
    )(w_in, *plain)
    wa2 = jnp.zeros((LANES, GLA_DK), F32).at[SMALL_A1:SMALL_A1 + GLA_GATE_RANK].set(gla_w_a2[0]).astype(BF16)
    dskip = jnp.broadcast_to(ssd_D[0].astype(F32)[:, None], (SSD_HEADS, tile))

    tok_spec = pl.BlockSpec((None, tile, D_MODEL), lambda b, s: (b, s, 0))

    mixer_inputs = [
        (x, tok_spec),
        (_row(norm_mix[0]), _resident((1, D_MODEL))),
        (w_in_r, _resident((D_MODEL, IN_WIDTH_PADDED))),
        (ssd_conv_w[0].reshape(SSD_CONV, SSD_CONV_CH), _resident((SSD_CONV, SSD_CONV_CH))),
        (_row(ssd_conv_b[0]), _resident((1, SSD_CONV_CH))),
        (_row(ssd_dt_bias[0], LANES), _resident((1, LANES))),
        (_row(ssd_A_log[0], LANES), _resident((1, LANES))),
        (dskip, _resident((SSD_HEADS, tile))),
        (_row(ssd_norm[0]), _resident((1, SSD_D_INNER))),
        (wa2, _resident((LANES, GLA_DK))),
        (_row(gla_b_a[0]), _resident((1, GLA_DK))),
        (_row(gla_norm[0]), _resident((1, GLA_HEAD_V))),
        (w_up_ssd_b, _resident((SSD_D_INNER, D_MODEL))),
        (w_up_gla_b, _resident((GLA_DV, D_MODEL))),
        (w_o_b, _resident((D_MODEL, D_MODEL))),
    ]
    mixer_scratch = [
        pltpu.VMEM((tile, D_MODEL), BF16),
        pltpu.VMEM((tile, SSD_D_INNER), F32),
        pltpu.VMEM((tile + CONV_HALO, SSD_CONV_CH), F32),
        pltpu.VMEM((tile, GLA_DK), F32),
        pltpu.VMEM((tile, GLA_DK), F32),
        pltpu.VMEM((tile, GLA_DV), BF16),
        pltpu.VMEM((tile, GLA_DV), F32),
        pltpu.VMEM((tile, D_MODEL), F32),
        pltpu.VMEM((tile, D_MODEL), F32),
        pltpu.VMEM((tile, SSD_D_INNER), F32),
        pltpu.VMEM((tile, SSD_GROUPS * SSD_STATE), BF16),
        pltpu.VMEM((tile, SSD_GROUPS * SSD_STATE), BF16),
        pltpu.VMEM((tile, LANES), F32),
        pltpu.VMEM((tile, LANES), F32),
        pltpu.VMEM((tile, GLA_DK), F32),
        pltpu.VMEM((SSD_D_INNER, tile), F32),
        pltpu.VMEM((SSD_HEADS_PER_GROUP * SSD_HEAD_DIM, tile), BF16),
        pltpu.VMEM((tile, SSD_D_INNER), BF16),
        pltpu.VMEM((tile, GLA_DK), BF16),
        pltpu.VMEM((tile, GLA_DK), BF16),
        pltpu.VMEM((tile, GLA_DK), BF16),
        pltpu.VMEM((tile // CHUNK * GLA_HEADS, CHUNK, CHUNK), BF16),
        pltpu.VMEM((tile // CHUNK, GLA_DK, GLA_HEAD_V), BF16),
        pltpu.VMEM((tile, GLA_DV), F32),
        pltpu.VMEM((tile, GLA_DV), BF16),
        pltpu.VMEM((SSD_D_INNER, SSD_STATE), F32),
        pltpu.VMEM((GLA_DK, GLA_HEAD_V), F32),
    ]
    h1 = pl.pallas_call(
        functools.partial(_mixer_kernel, tile=tile),
        grid=grid,
        in_specs=[spec for _, spec in mixer_inputs],
        out_specs=tok_spec,
        out_shape=jax.ShapeDtypeStruct(x.shape, F32),
        scratch_shapes=mixer_scratch,
        compiler_params=pltpu.CompilerParams(
            dimension_semantics=("arbitrary", "arbitrary"), vmem_limit_bytes=VMEM_LIMIT_BYTES),
        name="mixer",
    )(*[a for a, _ in mixer_inputs])

    att_width = XATTN_HEADS * mem_len
    mem_spec = pl.BlockSpec((None, mem_len, D_MODEL), lambda b: (b, 0, 0))
    whole = lambda shape: pl.BlockSpec(shape, lambda b: (0,) * len(shape))
    wqk, vwo = pl.pallas_call(
        _memkv_kernel,
        grid=(batch,),
        in_specs=[mem_spec, whole((1, D_MODEL)), whole((D_MODEL, 2 * D_MODEL)),
                  whole((D_MODEL, D_MODEL)), whole((D_MODEL, D_MODEL))],
        out_specs=[pl.BlockSpec((None, D_MODEL, att_width), lambda b: (b, 0, 0)),
                   pl.BlockSpec((None, att_width, D_MODEL), lambda b: (b, 0, 0))],
        out_shape=[jax.ShapeDtypeStruct((batch, D_MODEL, att_width), BF16),
                   jax.ShapeDtypeStruct((batch, att_width, D_MODEL), BF16)],
        compiler_params=pltpu.CompilerParams(
            dimension_semantics=("arbitrary",), vmem_limit_bytes=VMEM_LIMIT_BYTES),
        name="memkv",
    )(mem, _row(norm_mem[0]), w_xkv_b, w_xq_b, w_xo_b)

    assert seq % tail_tile == 0
    tail_spec = pl.BlockSpec((None, tail_tile, D_MODEL), lambda b, s: (b, s, 0))
    out = pl.pallas_call(
        _tail_kernel,
        grid=(batch, seq // tail_tile),
        in_specs=[tail_spec,
                  _resident((1, D_MODEL)),
                  pl.BlockSpec((None, D_MODEL, att_width), lambda b, s: (b, 0, 0)),
                  pl.BlockSpec((None, att_width, D_MODEL), lambda b, s: (b, 0, 0)),
                  _resident((1, D_MODEL)),
                  _resident((D_MODEL, 2 * D_FF)),
                  _resident((D_FF, D_MODEL)),
                  _resident((1, D_MODEL))],
        out_specs=tail_spec,
        out_shape=jax.ShapeDtypeStruct(x.shape, F32),
        scratch_shapes=[pltpu.VMEM((tail_tile, att_width), BF16),
                        pltpu.VMEM((tail_tile, D_FF), BF16)],
        compiler_params=pltpu.CompilerParams(
            dimension_semantics=("arbitrary", "arbitrary"), vmem_limit_bytes=VMEM_LIMIT_BYTES),
        name="tail",
    )(h1, _row(norm_xattn[0]), wqk, vwo, _row(norm_ffn[0]), w_ffn_in_b, w_ffn_out_b, _row(norm_final))
    return out


def kernel(x, mem, norm_mix, w_in, ssd_conv_w, ssd_conv_b, ssd_dt_bias, ssd_A_log, ssd_D, ssd_norm,
           gla_w_a2, gla_b_a, gla_norm, w_up_ssd, w_up_gla, w_o, norm_xattn, norm_mem, w_xq, w_xkv,
           w_xo, norm_ffn, w_ffn_in, w_ffn_out, norm_final):
    return _forward(x, mem, norm_mix, w_in, ssd_conv_w, ssd_conv_b, ssd_dt_bias, ssd_A_log, ssd_D,
                    ssd_norm, gla_w_a2, gla_b_a, gla_norm, w_up_ssd, w_up_gla, w_o, norm_xattn,
                    norm_mem, w_xq, w_xkv, w_xo, norm_ffn, w_ffn_in, w_ffn_out, norm_final,
                    tile=MIXER_TILE, tail_tile=TAIL_TILE)
```

```python
import functools

import jax
import jax.numpy as jnp
from jax import lax
from jax.experimental import pallas as pl
from jax.experimental.pallas import tpu as pltpu

F32 = jnp.float32
BF16 = jnp.bfloat16

D_MODEL = 1024
EPS = 1e-6
CHUNK = 64
SSD_D_INNER = 1024
SSD_HEAD_DIM = 64
SSD_HEADS = 16
SSD_GROUPS = 2
SSD_HEADS_PER_GROUP = SSD_HEADS // SSD_GROUPS
SSD_STATE = 128
SSD_CONV = 4
SSD_CONV_CH = SSD_D_INNER + 2 * SSD_GROUPS * SSD_STATE
GLA_HEADS = 4
GLA_DK = 512
GLA_DV = 1024
GLA_HEAD_K = GLA_DK // GLA_HEADS
GLA_HEAD_V = GLA_DV // GLA_HEADS
GLA_GATE_RANK = 16
GLA_TAU = 16.0
XATTN_HEADS = 4
XATTN_HEAD_DIM = D_MODEL // XATTN_HEADS
D_FF = 2816
IN_SIZES = (SSD_D_INNER, SSD_CONV_CH, SSD_HEADS, GLA_DK, GLA_DK, GLA_DV, GLA_DV, GLA_GATE_RANK,
            D_MODEL, D_MODEL)

LANES = 128
SUBLANES = 8
VMEM_LIMIT_BYTES = 56 * 1024 * 1024
MIXER_TILE = 256
TAIL_TILE = 1024
TAIL_ROW_GROUPS = 4
PREP_STEPS = 16
PREP_SLAB = 512

OFF_Z = 0
OFF_XBC = OFF_Z + SSD_D_INNER
OFF_Q = OFF_XBC + SSD_CONV_CH
OFF_K = OFF_Q + GLA_DK
OFF_V = OFF_K + GLA_DK
OFF_R = OFF_V + GLA_DV
OFF_GS = OFF_R + GLA_DV
OFF_GG = OFF_GS + D_MODEL
OFF_SMALL = OFF_GG + D_MODEL
IN_WIDTH_PADDED = OFF_SMALL + PREP_SLAB
SMALL_DT = 0
SMALL_A1 = SSD_HEADS

CONV_HALO = SUBLANES


def _dot(a, b):
    return jnp.dot(a, b, preferred_element_type=F32)


def _dot_nt(a, b):
    return lax.dot_general(a, b, (((1,), (1,)), ((), ())), preferred_element_type=F32)


def _dot_tn(a, b):
    return lax.dot_general(a, b, (((0,), (0,)), ((), ())), preferred_element_type=F32)


def _rms(x, g):
    return x * lax.rsqrt(jnp.mean(x * x, axis=-1, keepdims=True) + EPS) * g


def _silu(x):
    return x * jax.nn.sigmoid(x)


def _split3(x):
    hi = x.astype(BF16)
    r1 = x - hi.astype(F32)
    mid = r1.astype(BF16)
    lo = (r1 - mid.astype(F32)).astype(BF16)
    return hi, mid, lo


def _cumsum_rows(tri, x):
    hi, mid, lo = _split3(x)
    return _dot(tri, hi) + _dot(tri, mid) + _dot(tri, lo)


def _cumsum_lanes(x, upper):
    hi, mid, lo = _split3(x)
    return _dot(hi, upper) + _dot(mid, upper) + _dot(lo, upper)


def _mixer_kernel(x_ref, gmix_ref, win_ref, convw_ref, convb_ref, dtb_ref, alog_ref, dskip_ref,
                  ssdnorm_ref, wa2_ref, ba_ref, glanorm_ref, wus_ref, wug_ref, wo_ref,
                  h_ref,
                  n_ref, z_ref, xpad_ref, q_ref, k_ref, v_ref, r_ref, gs_ref, gg_ref,
                  xs_ref, b_ref, c_ref, dt_ref, a_ref, la_ref, yt_ref, xd_ref, ys_ref,
                  qt_ref, kt_ref, kh_ref, att_ref, sb_ref, o_ref, yg_ref,
                  sstate_ref, gstate_ref, *, tile):
    s = pl.program_id(1)

    @pl.when(s == 0)
    def _():
        sstate_ref[...] = jnp.zeros_like(sstate_ref)
        gstate_ref[...] = jnp.zeros_like(gstate_ref)
        xpad_ref[0:CONV_HALO, :] = jnp.zeros((CONV_HALO, SSD_CONV_CH), F32)

    n_ref[...] = _rms(x_ref[...], gmix_ref[...]).astype(BF16)

    def proj(off, width):
        return _dot(n_ref[...], win_ref[:, off:off + width])

    slab = 512
    deferred = [(dst, off, c0)
                for dst, off, width in ((q_ref, OFF_Q, GLA_DK), (k_ref, OFF_K, GLA_DK),
                                        (v_ref, OFF_V, GLA_DV), (r_ref, OFF_R, GLA_DV),
                                        (z_ref, OFF_Z, SSD_D_INNER), (gs_ref, OFF_GS, D_MODEL),
                                        (gg_ref, OFF_GG, D_MODEL))
                for c0 in range(0, width, slab)]

    def emit_proj(count=1):
        for _ in range(count):
            if deferred:
                dst, off, c0 = deferred.pop(0)
                dst[:, c0:c0 + slab] = proj(off + c0, slab).astype(dst.dtype)

    xpad_ref[CONV_HALO:CONV_HALO + tile, :] = proj(OFF_XBC, SSD_CONV_CH)
    small = proj(OFF_SMALL, LANES)

    dt = jax.nn.softplus(small + dtb_ref[...])
    dt_ref[...] = dt
    a_ref[...] = dt * (-jnp.exp(alog_ref[...]))
    logits = _dot(small.astype(BF16), wa2_ref[...]) + ba_ref[...]
    la_ref[...] = jax.nn.log_sigmoid(logits) / GLA_TAU

    col_block = SSD_GROUPS * SSD_STATE
    for cb in range(SSD_CONV_CH // col_block):
        cols = slice(cb * col_block, (cb + 1) * col_block)
        acc = convb_ref[:, cols]
        for j in range(SSD_CONV):
            lo = CONV_HALO - (SSD_CONV - 1) + j
            acc = acc + convw_ref[j:j + 1, cols] * xpad_ref[lo:lo + tile, cols]
        act = _silu(acc)
        if cb < SSD_D_INNER // col_block:
            xs_ref[:, cols] = act
        elif cb == SSD_D_INNER // col_block:
            b_ref[...] = act.astype(BF16)
        else:
            c_ref[...] = act.astype(BF16)
        emit_proj()
    xpad_ref[0:CONV_HALO, :] = xpad_ref[tile:tile + CONV_HALO, :]

    rid = lax.broadcasted_iota(jnp.int32, (tile, tile), 0)
    cid = lax.broadcasted_iota(jnp.int32, (tile, tile), 1)
    upper = rid <= cid
    a_t = a_ref[...].T
    acs_t = _cumsum_lanes(a_t, upper.astype(BF16))
    acs = acs_t.T
    dt_t = dt_ref[...].T
    xs_t = xs_ref[...].T
    hp = SSD_HEADS_PER_GROUP * SSD_HEAD_DIM
    for g in range(SSD_GROUPS):
        nsl = slice(g * SSD_STATE, (g + 1) * SSD_STATE)
        bg = b_ref[:, nsl]
        cg = c_ref[:, nsl]
        cb_t = jnp.where(upper, _dot_nt(bg, cg), 0.0)
        state = sstate_ref[g * hp:(g + 1) * hp, :]
        y_off_t = _dot_nt(state.astype(BF16), cg)
        for r in range(SSD_HEADS_PER_GROUP):
            h = g * SSD_HEADS_PER_GROUP + r
            psl = slice(h * SSD_HEAD_DIM, (h + 1) * SSD_HEAD_DIM)
            rsl = slice(r * SSD_HEAD_DIM, (r + 1) * SSD_HEAD_DIM)
            col = acs[:, h:h + 1]
            row = acs_t[h:h + 1, :]
            last = row[:, tile - 1:tile]
            m_t = (jnp.exp(jnp.minimum(row - col, 0.0)) * cb_t).astype(BF16)
            xh_t = xs_t[psl, :]
            xdt_t = xh_t * dt_t[h:h + 1, :]
            y_diag_t = _dot(xdt_t.astype(BF16), m_t)
            yt_ref[psl, :] = (y_diag_t + y_off_t[rsl, :] * jnp.exp(row)
                              + dskip_ref[h:h + 1, :] * xh_t)
            xd_ref[rsl, :] = (xdt_t * jnp.exp(last - row)).astype(BF16)
            sstate_ref[psl, :] = state[rsl, :] * jnp.exp(last)
            emit_proj()
        sstate_ref[g * hp:(g + 1) * hp, :] += _dot(xd_ref[...], bg)
    emit_proj(len(deferred))
    yz = yt_ref[...].T * _silu(z_ref[...])
    gn = SSD_D_INNER // SSD_GROUPS
    for g in range(SSD_GROUPS):
        csl = slice(g * gn, (g + 1) * gn)
        ys_ref[:, csl] = _rms(yz[:, csl], ssdnorm_ref[:, csl]).astype(BF16)
    merged_ssd = jax.nn.sigmoid(gs_ref[...]) * _dot(ys_ref[...], wus_ref[...])

    n_chunks = tile // CHUNK
    chunk_of = lambda idx: lax.shift_right_logical(idx, CHUNK.bit_length() - 1)
    tri_blocks = ((rid >= cid) & (chunk_of(rid) == chunk_of(cid))).astype(BF16)
    bcum = _cumsum_rows(tri_blocks, la_ref[...])
    blast = jnp.concatenate(
        [jnp.broadcast_to(bcum[(i + 1) * CHUNK - 1:(i + 1) * CHUNK, :], (CHUNK, GLA_DK))
         for i in range(n_chunks)], axis=0)
    kk = k_ref[...]
    qt_ref[...] = (q_ref[...] * (GLA_HEAD_K ** -0.5) * jnp.exp(bcum)).astype(BF16)
    kt_ref[...] = (kk * jnp.exp(-bcum)).astype(BF16)
    kh_ref[...] = (kk * jnp.exp(blast - bcum)).astype(BF16)
    dec_t = jnp.exp(blast).T
    causal = (lax.broadcasted_iota(jnp.int32, (CHUNK, CHUNK), 0)
              >= lax.broadcasted_iota(jnp.int32, (CHUNK, CHUNK), 1))
    ksls = [slice(j * GLA_HEAD_K, (j + 1) * GLA_HEAD_K) for j in range(GLA_HEADS)]
    vsls = [slice(j * GLA_HEAD_V, (j + 1) * GLA_HEAD_V) for j in range(GLA_HEADS)]
    for i in range(n_chunks):
        rs = slice(i * CHUNK, (i + 1) * CHUNK)
        for j in range(GLA_HEADS):
            scores = _dot_nt(qt_ref[rs, ksls[j]], kt_ref[rs, ksls[j]])
            att_ref[i * GLA_HEADS + j] = jnp.where(causal, scores, 0.0).astype(BF16)
    states = [gstate_ref[ksl, :] for ksl in ksls]
    for i in range(n_chunks):
        rs = slice(i * CHUNK, (i + 1) * CHUNK)
        for j in range(GLA_HEADS):
            sb_ref[i, ksls[j], :] = states[j].astype(BF16)
            update = _dot_tn(kh_ref[rs, ksls[j]], v_ref[rs, vsls[j]])
            states[j] = states[j] * dec_t[ksls[j], i * CHUNK:i * CHUNK + 1] + update
    for j in range(GLA_HEADS):
        gstate_ref[ksls[j], :] = states[j]
    for i in range(n_chunks):
        rs = slice(i * CHUNK, (i + 1) * CHUNK)
        for j in range(GLA_HEADS):
            o_ref[rs, vsls[j]] = (_dot(att_ref[i * GLA_HEADS + j], v_ref[rs, vsls[j]])
                                  + _dot(qt_ref[rs, ksls[j]], sb_ref[i, ksls[j], :]))
    for j in range(GLA_HEADS):
        yg_ref[:, vsls[j]] = (_rms(o_ref[:, vsls[j]], glanorm_ref[...])
                              * _silu(r_ref[:, vsls[j]])).astype(BF16)

    merged = merged_ssd + jax.nn.sigmoid(gg_ref[...]) * _dot(yg_ref[...], wug_ref[...])
    h_ref[...] = x_ref[...] + _dot(merged.astype(BF16), wo_ref[...])


def _memkv_kernel(mem_ref, g_ref, wkv_ref, wq_ref, wo_ref, wqk_ref, vwo_ref):
    mem_len = mem_ref.shape[0]
    m = _rms(mem_ref[...], g_ref[...]).astype(BF16)
    kv = _dot(m, wkv_ref[...])
    for j in range(XATTN_HEADS):
        dsl = slice(j * XATTN_HEAD_DIM, (j + 1) * XATTN_HEAD_DIM)
        msl = slice(j * mem_len, (j + 1) * mem_len)
        k_j = kv[:, dsl].astype(BF16)
        v_j = kv[:, D_MODEL + j * XATTN_HEAD_DIM:D_MODEL + (j + 1) * XATTN_HEAD_DIM].astype(BF16)
        wqk_ref[:, msl] = (_dot_nt(wq_ref[:, dsl], k_j) * (XATTN_HEAD_DIM ** -0.5)).astype(BF16)
        vwo_ref[msl, :] = _dot(v_j, wo_ref[dsl, :]).astype(BF16)


def _tail_kernel(h_ref, gx_ref, wqk_ref, vwo_ref, gf_ref, wfi_ref, wfo_ref, gfin_ref,
                 out_ref, p_ref, act_ref):
    mem_len = wqk_ref.shape[1] // XATTN_HEADS
    ff_block = D_FF // 2
    rows_per_group = h_ref.shape[0] // TAIL_ROW_GROUPS
    groups = [slice(g * rows_per_group, (g + 1) * rows_per_group) for g in range(TAIL_ROW_GROUPS)]
    h1 = [h_ref[rows, :] for rows in groups]
    sc = [_dot(_rms(h, gx_ref[...]).astype(BF16), wqk_ref[...]) for h in h1]
    for rows, s in zip(groups, sc):
        for j in range(XATTN_HEADS):
            msl = slice(j * mem_len, (j + 1) * mem_len)
            e = jnp.exp(s[:, msl] - jnp.max(s[:, msl], axis=-1, keepdims=True))
            p_ref[rows, msl] = (e / jnp.sum(e, axis=-1, keepdims=True)).astype(BF16)
    h2 = [h + _dot(p_ref[rows, :], vwo_ref[...]) for rows, h in zip(groups, h1)]
    n3 = [_rms(h, gf_ref[...]).astype(BF16) for h in h2]
    for cb in range(D_FF // ff_block):
        for rows, n in zip(groups, n3):
            gate = _dot(n, wfi_ref[:, cb * ff_block:(cb + 1) * ff_block])
            up = _dot(n, wfi_ref[:, D_FF + cb * ff_block:D_FF + (cb + 1) * ff_block])
            act_ref[rows, cb * ff_block:(cb + 1) * ff_block] = (_silu(gate) * up).astype(BF16)
    for rows, h in zip(groups, h2):
        h3 = h + _dot(act_ref[rows, :], wfo_ref[...])
        out_ref[rows, :] = _rms(h3, gfin_ref[...])


IN_BOUNDS = tuple(sum(IN_SIZES[:i]) for i in range(len(IN_SIZES) + 1))
IN_RUNS = ((OFF_Z, IN_BOUNDS[0], IN_BOUNDS[2] - IN_BOUNDS[0]),
           (OFF_Q, IN_BOUNDS[3], IN_BOUNDS[7] - IN_BOUNDS[3]),
           (OFF_GS, IN_BOUNDS[8], IN_BOUNDS[10] - IN_BOUNDS[8]))
IN_DT_COL = IN_BOUNDS[2]
IN_A1_COL = IN_BOUNDS[7]
assert all(dst % PREP_SLAB == 0 and width % PREP_SLAB == 0 for dst, _, width in IN_RUNS)
assert OFF_SMALL == (PREP_STEPS - 1) * PREP_SLAB and IN_WIDTH_PADDED == PREP_STEPS * PREP_SLAB


def _prep_src_row(i):
    unit = SSD_HEADS
    start = i * (PREP_SLAB // unit)
    prev_shift = 0
    for dst, src, _ in IN_RUNS:
        start = start + jnp.where(i >= dst // PREP_SLAB, (src - dst - prev_shift) // unit, 0)
        prev_shift = src - dst
    last = (IN_BOUNDS[-1] - PREP_SLAB) // unit
    return jnp.minimum(start, last) * unit


def _prep_kernel(wint_ref, dt_ref, a1_ref, *refs):
    n_plain = (len(refs) - 1) // 2
    plain_in, owin_ref, plain_out = refs[:n_plain], refs[n_plain], refs[n_plain + 1:]
    i = pl.program_id(0)

    @pl.when(i < PREP_STEPS - 1)
    def _():
        owin_ref[...] = wint_ref[...].T.astype(BF16)

    @pl.when(i == PREP_STEPS - 1)
    def _():
        pad = jnp.zeros((LANES - SSD_HEADS - GLA_GATE_RANK, D_MODEL), F32)
        small_t = jnp.concatenate([dt_ref[...], a1_ref[...], pad], axis=0)
        owin_ref[:, 0:LANES] = small_t.T.astype(BF16)
        owin_ref[:, LANES:] = jnp.zeros((D_MODEL, PREP_SLAB - LANES), BF16)

    for src_ref, dst_ref in zip(plain_in, plain_out):
        dst_ref[...] = src_ref[...].astype(BF16)


def _resident(shape):
    return pl.BlockSpec(shape, lambda b, s: (0,) * len(shape), pipeline_mode=pl.Buffered(1))


def _row(v, width=None):
    v = v.reshape(1, -1).astype(F32)
    if width is not None and v.shape[1] < width:
        v = jnp.pad(v, ((0, 0), (0, width - v.shape[1])))
    return v


@functools.partial(jax.jit, static_argnames=("tile", "tail_tile"))
def _forward(x, mem, norm_mix, w_in, ssd_conv_w, ssd_conv_b, ssd_dt_bias, ssd_A_log, ssd_D, ssd_norm,
             gla_w_a2, gla_b_a, gla_norm, w_up_ssd, w_up_gla, w_o, norm_xattn, norm_mem, w_xq, w_xkv,
             w_xo, norm_ffn, w_ffn_in, w_ffn_out, norm_final, *, tile, tail_tile):
    batch, seq, _ = x.shape
    mem_len = mem.shape[1]
    assert seq % tile == 0 and tile % CHUNK == 0
    grid = (batch, seq // tile)

    w_in_t = jnp.swapaxes(w_in[0], 0, 1)
    plain = [w_up_ssd, w_up_gla, w_o, w_xq, w_xkv, w_xo, w_ffn_in, w_ffn_out]
    row_block_in = lambda a: pl.BlockSpec((None, a.shape[1] // PREP_STEPS, a.shape[2]),
                                          lambda i: (0, i, 0))
    row_block_out = lambda a: pl.BlockSpec((a.shape[0] // PREP_STEPS, a.shape[1]), lambda i: (i, 0))
    rows_at = lambda n, start: pl.BlockSpec((pl.Element(n), pl.Element(D_MODEL)),
                                            lambda i: (start(i), 0))
    outs = [jax.ShapeDtypeStruct(a.shape[1:], BF16) for a in plain]
    (w_in_r, w_up_ssd_b, w_up_gla_b, w_o_b, w_xq_b, w_xkv_b, w_xo_b, w_ffn_in_b,
     w_ffn_out_b) = pl.pallas_call(
        _prep_kernel,
        grid=(PREP_STEPS,),
        in_specs=([rows_at(PREP_SLAB, _prep_src_row),
                   rows_at(SSD_HEADS, lambda i: IN_DT_COL), rows_at(GLA_GATE_RANK, lambda i: IN_A1_COL)]
                  + [row_block_in(a) for a in plain]),
        out_specs=([pl.BlockSpec((D_MODEL, PREP_SLAB), lambda i: (0, i))]
                   + [row_block_out(a) for a in outs]),
        out_shape=[jax.ShapeDtypeStruct((D_MODEL, IN_WIDTH_PADDED), BF16)] + outs,
        compiler_params=pltpu.CompilerParams(
            dimension_semantics=("arbitrary",), vmem_limit_bytes=VMEM_LIMIT_BYTES),
        name="prep",
    )(w_in_t, w_in_t, w_in_t, *plain)
    wa2 = jnp.zeros((LANES, GLA_DK), F32).at[SMALL_A1:SMALL_A1 + GLA_GATE_RANK].set(gla_w_a2[0]).astype(BF16)
    dskip = jnp.broadcast_to(ssd_D[0].astype(F32)[:, None], (SSD_HEADS, tile))

    tok_spec = pl.BlockSpec((None, tile, D_MODEL), lambda b, s: (b, s, 0))

    mixer_inputs = [
        (x, tok_spec),
        (_row(norm_mix[0]), _resident((1, D_MODEL))),
        (w_in_r, _resident((D_MODEL, IN_WIDTH_PADDED))),
        (ssd_conv_w[0].reshape(SSD_CONV, SSD_CONV_CH), _resident((SSD_CONV, SSD_CONV_CH))),
        (_row(ssd_conv_b[0]), _resident((1, SSD_CONV_CH))),
        (_row(ssd_dt_bias[0], LANES), _resident((1, LANES))),
        (_row(ssd_A_log[0], LANES), _resident((1, LANES))),
        (dskip, _resident((SSD_HEADS, tile))),
        (_row(ssd_norm[0]), _resident((1, SSD_D_INNER))),
        (wa2, _resident((LANES, GLA_DK))),
        (_row(gla_b_a[0]), _resident((1, GLA_DK))),
        (_row(gla_norm[0]), _resident((1, GLA_HEAD_V))),
        (w_up_ssd_b, _resident((SSD_D_INNER, D_MODEL))),
        (w_up_gla_b, _resident((GLA_DV, D_MODEL))),
        (w_o_b, _resident((D_MODEL, D_MODEL))),
    ]
    mixer_scratch = [
        pltpu.VMEM((tile, D_MODEL), BF16),
        pltpu.VMEM((tile, SSD_D_INNER), F32),
        pltpu.VMEM((tile + CONV_HALO, SSD_CONV_CH), F32),
        pltpu.VMEM((tile, GLA_DK), F32),
        pltpu.VMEM((tile, GLA_DK), F32),
        pltpu.VMEM((tile, GLA_DV), BF16),
        pltpu.VMEM((tile, GLA_DV), F32),
        pltpu.VMEM((tile, D_MODEL), F32),
        pltpu.VMEM((tile, D_MODEL), F32),
        pltpu.VMEM((tile, SSD_D_INNER), F32),
        pltpu.VMEM((tile, SSD_GROUPS * SSD_STATE), BF16),
        pltpu.VMEM((tile, SSD_GROUPS * SSD_STATE), BF16),
        pltpu.VMEM((tile, LANES), F32),
        pltpu.VMEM((tile, LANES), F32),
        pltpu.VMEM((tile, GLA_DK), F32),
        pltpu.VMEM((SSD_D_INNER, tile), F32),
        pltpu.VMEM((SSD_HEADS_PER_GROUP * SSD_HEAD_DIM, tile), BF16),
        pltpu.VMEM((tile, SSD_D_INNER), BF16),
        pltpu.VMEM((tile, GLA_DK), BF16),
        pltpu.VMEM((tile, GLA_DK), BF16),
        pltpu.VMEM((tile, GLA_DK), BF16),
        pltpu.VMEM((tile // CHUNK * GLA_HEADS, CHUNK, CHUNK), BF16),
        pltpu.VMEM((tile // CHUNK, GLA_DK, GLA_HEAD_V), BF16),
        pltpu.VMEM((tile, GLA_DV), F32),
        pltpu.VMEM((tile, GLA_DV), BF16),
        pltpu.VMEM((SSD_D_INNER, SSD_STATE), F32),
        pltpu.VMEM((GLA_DK, GLA_HEAD_V), F32),
    ]
    h1 = pl.pallas_call(
        functools.partial(_mixer_kernel, tile=tile),
        grid=grid,
        in_specs=[spec for _, spec in mixer_inputs],
        out_specs=tok_spec,
        out_shape=jax.ShapeDtypeStruct(x.shape, F32),
        scratch_shapes=mixer_scratch,
        compiler_params=pltpu.CompilerParams(
            dimension_semantics=("arbitrary", "arbitrary"), vmem_limit_bytes=VMEM_LIMIT_BYTES),
        name="mixer",
    )(*[a for a, _ in mixer_inputs])

    att_width = XATTN_HEADS * mem_len
    mem_spec = pl.BlockSpec((None, mem_len, D_MODEL), lambda b: (b, 0, 0))
    whole = lambda shape: pl.BlockSpec(shape, lambda b: (0,) * len(shape))
    wqk, vwo = pl.pallas_call(
        _memkv_kernel,
        grid=(batch,),
        in_specs=[mem_spec, whole((1, D_MODEL)), whole((D_MODEL, 2 * D_MODEL)),
                  whole((D_MODEL, D_MODEL)), whole((D_MODEL, D_MODEL))],
        out_specs=[pl.BlockSpec((None, D_MODEL, att_width), lambda b: (b, 0, 0)),
                   pl.BlockSpec((None, att_width, D_MODEL), lambda b: (b, 0, 0))],
        out_shape=[jax.ShapeDtypeStruct((batch, D_MODEL, att_width), BF16),
                   jax.ShapeDtypeStruct((batch, att_width, D_MODEL), BF16)],
        compiler_params=pltpu.CompilerParams(
            dimension_semantics=("arbitrary",), vmem_limit_bytes=VMEM_LIMIT_BYTES),
        name="memkv",
    )(mem, _row(norm_mem[0]), w_xkv_b, w_xq_b, w_xo_b)

    assert seq % tail_tile == 0
    tail_spec = pl.BlockSpec((None, tail_tile, D_MODEL), lambda b, s: (b, s, 0))
    out = pl.pallas_call(
        _tail_kernel,
        grid=(batch, seq // tail_tile),
        in_specs=[tail_spec,
                  _resident((1, D_MODEL)),
                  pl.BlockSpec((None, D_MODEL, att_width), lambda b, s: (b, 0, 0)),
                  pl.BlockSpec((None, att_width, D_MODEL), lambda b, s: (b, 0, 0)),
                  _resident((1, D_MODEL)),
                  _resident((D_MODEL, 2 * D_FF)),
                  _resident((D_FF, D_MODEL)),
                  _resident((1, D_MODEL))],
        out_specs=tail_spec,
        out_shape=jax.ShapeDtypeStruct(x.shape, F32),
        scratch_shapes=[pltpu.VMEM((tail_tile, att_width), BF16),
                        pltpu.VMEM((tail_tile, D_FF), BF16)],
        compiler_params=pltpu.CompilerParams(
            dimension_semantics=("arbitrary", "arbitrary"), vmem_limit_bytes=VMEM_LIMIT_BYTES),
        name="tail",
    )(h1, _row(norm_xattn[0]), wqk, vwo, _row(norm_ffn[0]), w_ffn_in_b, w_ffn_out_b, _row(norm_final))
    return out


def kernel(x, mem, norm_mix, w_in, ssd_conv_w, ssd_conv_b, ssd_dt_bias, ssd_A_log, ssd_D, ssd_norm,
           gla_w_a2, gla_b_a, gla_norm, w_up_ssd, w_up_gla, w_o, norm_xattn, norm_mem, w_xq, w_xkv,
           w_xo, norm_ffn, w_ffn_in, w_ffn_out, norm_final):
    return _forward(x, mem, norm_mix, w_in, ssd_conv_w, ssd_conv_b, ssd_dt_bias, ssd_A_log, ssd_D,
                    ssd_norm, gla_w_a2, gla_b_a, gla_norm, w_up_ssd, w_up_gla, w_o, norm_xattn,
                    norm_mem, w_xq, w_xkv, w_xo, norm_ffn, w_ffn_in, w_ffn_out, norm_final,
                    tile=MIXER_TILE, tail_tile=TAIL_TILE)
```

```python
import functools

import jax
import jax.numpy as jnp
from jax import lax
from jax.experimental import pallas as pl
from jax.experimental.pallas import tpu as pltpu

F32 = jnp.float32
BF16 = jnp.bfloat16

D_MODEL = 1024
EPS = 1e-6
CHUNK = 64
SSD_D_INNER = 1024
SSD_HEAD_DIM = 64
SSD_HEADS = 16
SSD_GROUPS = 2
SSD_HEADS_PER_GROUP = SSD_HEADS // SSD_GROUPS
SSD_STATE = 128
SSD_CONV = 4
SSD_CONV_CH = SSD_D_INNER + 2 * SSD_GROUPS * SSD_STATE
GLA_HEADS = 4
GLA_DK = 512
GLA_DV = 1024
GLA_HEAD_K = GLA_DK // GLA_HEADS
GLA_HEAD_V = GLA_DV // GLA_HEADS
GLA_GATE_RANK = 16
GLA_TAU = 16.0
XATTN_HEADS = 4
XATTN_HEAD_DIM = D_MODEL // XATTN_HEADS
D_FF = 2816
IN_SIZES = (SSD_D_INNER, SSD_CONV_CH, SSD_HEADS, GLA_DK, GLA_DK, GLA_DV, GLA_DV, GLA_GATE_RANK,
            D_MODEL, D_MODEL)

LANES = 128
SUBLANES = 8
VMEM_LIMIT_BYTES = 56 * 1024 * 1024
MIXER_TILE = 256
TAIL_TILE = 1024
TAIL_ROW_GROUPS = 4
PREP_STEPS = 16
PREP_SLAB = 512

OFF_Z = 0
OFF_XBC = OFF_Z + SSD_D_INNER
OFF_Q = OFF_XBC + SSD_CONV_CH
OFF_K = OFF_Q + GLA_DK
OFF_V = OFF_K + GLA_DK
OFF_R = OFF_V + GLA_DV
OFF_GS = OFF_R + GLA_DV
OFF_GG = OFF_GS + D_MODEL
OFF_SMALL = OFF_GG + D_MODEL
IN_WIDTH_PADDED = OFF_SMALL + PREP_SLAB
SMALL_DT = 0
SMALL_A1 = SSD_HEADS

CONV_HALO = SUBLANES


def _dot(a, b):
    return jnp.dot(a, b, preferred_element_type=F32)


def _dot_nt(a, b):
    return lax.dot_general(a, b, (((1,), (1,)), ((), ())), preferred_element_type=F32)


def _dot_tn(a, b):
    return lax.dot_general(a, b, (((0,), (0,)), ((), ())), preferred_element_type=F32)


def _rms(x, g):
    return x * lax.rsqrt(jnp.mean(x * x, axis=-1, keepdims=True) + EPS) * g


def _silu(x):
    return x * jax.nn.sigmoid(x)


def _split3(x):
    hi = x.astype(BF16)
    r1 = x - hi.astype(F32)
    mid = r1.astype(BF16)
    lo = (r1 - mid.astype(F32)).astype(BF16)
    return hi, mid, lo


def _cumsum_rows(tri, x):
    hi, mid, lo = _split3(x)
    return _dot(tri, hi) + _dot(tri, mid) + _dot(tri, lo)


def _cumsum_lanes(x, upper):
    hi, mid, lo = _split3(x)
    return _dot(hi, upper) + _dot(mid, upper) + _dot(lo, upper)


def _mixer_kernel(x_ref, gmix_ref, win_ref, convw_ref, convb_ref, dtb_ref, alog_ref, dskip_ref,
                  ssdnorm_ref, wa2_ref, ba_ref, glanorm_ref, wus_ref, wug_ref, wo_ref,
                  h_ref,
                  n_ref, z_ref, xpad_ref, q_ref, k_ref, v_ref, r_ref, gs_ref, gg_ref,
                  xs_ref, b_ref, c_ref, dt_ref, a_ref, la_ref, yt_ref, xd_ref, ys_ref,
                  qt_ref, kt_ref, kh_ref, att_ref, sb_ref, o_ref, yg_ref,
                  sstate_ref, gstate_ref, *, tile):
    s = pl.program_id(1)

    @pl.when(s == 0)
    def _():
        sstate_ref[...] = jnp.zeros_like(sstate_ref)
        gstate_ref[...] = jnp.zeros_like(gstate_ref)
        xpad_ref[0:CONV_HALO, :] = jnp.zeros((CONV_HALO, SSD_CONV_CH), F32)

    n_ref[...] = _rms(x_ref[...], gmix_ref[...]).astype(BF16)

    def proj(off, width):
        return _dot(n_ref[...], win_ref[:, off:off + width])

    slab = 512
    deferred = [(dst, off, c0)
                for dst, off, width in ((q_ref, OFF_Q, GLA_DK), (k_ref, OFF_K, GLA_DK),
                                        (v_ref, OFF_V, GLA_DV), (r_ref, OFF_R, GLA_DV),
                                        (z_ref, OFF_Z, SSD_D_INNER), (gs_ref, OFF_GS, D_MODEL),
                                        (gg_ref, OFF_GG, D_MODEL))
                for c0 in range(0, width, slab)]

    def emit_proj(count=1):
        for _ in range(count):
            if deferred:
                dst, off, c0 = deferred.pop(0)
                dst[:, c0:c0 + slab] = proj(off + c0, slab).astype(dst.dtype)

    xpad_ref[CONV_HALO:CONV_HALO + tile, :] = proj(OFF_XBC, SSD_CONV_CH)
    small = proj(OFF_SMALL, LANES)

    dt = jax.nn.softplus(small + dtb_ref[...])
    dt_ref[...] = dt
    a_ref[...] = dt * (-jnp.exp(alog_ref[...]))
    logits = _dot(small.astype(BF16), wa2_ref[...]) + ba_ref[...]
    la_ref[...] = jax.nn.log_sigmoid(logits) / GLA_TAU

    col_block = SSD_GROUPS * SSD_STATE
    for cb in range(SSD_CONV_CH // col_block):
        cols = slice(cb * col_block, (cb + 1) * col_block)
        acc = convb_ref[:, cols]
        for j in range(SSD_CONV):
            lo = CONV_HALO - (SSD_CONV - 1) + j
            acc = acc + convw_ref[j:j + 1, cols] * xpad_ref[lo:lo + tile, cols]
        act = _silu(acc)
        if cb < SSD_D_INNER // col_block:
            xs_ref[:, cols] = act
        elif cb == SSD_D_INNER // col_block:
            b_ref[...] = act.astype(BF16)
        else:
            c_ref[...] = act.astype(BF16)
        emit_proj()
    xpad_ref[0:CONV_HALO, :] = xpad_ref[tile:tile + CONV_HALO, :]

    rid = lax.broadcasted_iota(jnp.int32, (tile, tile), 0)
    cid = lax.broadcasted_iota(jnp.int32, (tile, tile), 1)

    def gla_stages():
        n_chunks = tile // CHUNK
        chunk_of = lambda idx: lax.shift_right_logical(idx, CHUNK.bit_length() - 1)
        tri_blocks = ((rid >= cid) & (chunk_of(rid) == chunk_of(cid))).astype(BF16)
        bcum = _cumsum_rows(tri_blocks, la_ref[...])
        blast = jnp.concatenate(
            [jnp.broadcast_to(bcum[(i + 1) * CHUNK - 1:(i + 1) * CHUNK, :], (CHUNK, GLA_DK))
             for i in range(n_chunks)], axis=0)
        kk = k_ref[...]
        yield
        qt_ref[...] = (q_ref[...] * (GLA_HEAD_K ** -0.5) * jnp.exp(bcum)).astype(BF16)
        yield
        kt_ref[...] = (kk * jnp.exp(-bcum)).astype(BF16)
        yield
        kh_ref[...] = (kk * jnp.exp(blast - bcum)).astype(BF16)
        dec_t = jnp.exp(blast).T
        causal = (lax.broadcasted_iota(jnp.int32, (CHUNK, CHUNK), 0)
                  >= lax.broadcasted_iota(jnp.int32, (CHUNK, CHUNK), 1))
        ksls = [slice(j * GLA_HEAD_K, (j + 1) * GLA_HEAD_K) for j in range(GLA_HEADS)]
        vsls = [slice(j * GLA_HEAD_V, (j + 1) * GLA_HEAD_V) for j in range(GLA_HEADS)]
        for i in range(n_chunks):
            rs = slice(i * CHUNK, (i + 1) * CHUNK)
            for j in range(GLA_HEADS):
                scores = _dot_nt(qt_ref[rs, ksls[j]], kt_ref[rs, ksls[j]])
                att_ref[i * GLA_HEADS + j] = jnp.where(causal, scores, 0.0).astype(BF16)
            yield
        states = [gstate_ref[ksl, :] for ksl in ksls]
        for i in range(n_chunks):
            rs = slice(i * CHUNK, (i + 1) * CHUNK)
            for j in range(GLA_HEADS):
                sb_ref[i, ksls[j], :] = states[j].astype(BF16)
                update = _dot_tn(kh_ref[rs, ksls[j]], v_ref[rs, vsls[j]])
                states[j] = states[j] * dec_t[ksls[j], i * CHUNK:i * CHUNK + 1] + update
            yield
        for j in range(GLA_HEADS):
            gstate_ref[ksls[j], :] = states[j]
        for i in range(n_chunks):
            rs = slice(i * CHUNK, (i + 1) * CHUNK)
            for j in range(GLA_HEADS):
                o_ref[rs, vsls[j]] = (_dot(att_ref[i * GLA_HEADS + j], v_ref[rs, vsls[j]])
                                      + _dot(qt_ref[rs, ksls[j]], sb_ref[i, ksls[j], :]))
            yield
        for j in range(GLA_HEADS):
            yg_ref[:, vsls[j]] = (_rms(o_ref[:, vsls[j]], glanorm_ref[...])
                                  * _silu(r_ref[:, vsls[j]])).astype(BF16)
            yield

    gla = gla_stages()

    upper = rid <= cid
    a_t = a_ref[...].T
    acs_t = _cumsum_lanes(a_t, upper.astype(BF16))
    acs = acs_t.T
    dt_t = dt_ref[...].T
    xs_t = xs_ref[...].T
    hp = SSD_HEADS_PER_GROUP * SSD_HEAD_DIM
    for g in range(SSD_GROUPS):
        nsl = slice(g * SSD_STATE, (g + 1) * SSD_STATE)
        bg = b_ref[:, nsl]
        cg = c_ref[:, nsl]
        cb_t = jnp.where(upper, _dot_nt(bg, cg), 0.0)
        state = sstate_ref[g * hp:(g + 1) * hp, :]
        y_off_t = _dot_nt(state.astype(BF16), cg)
        for r in range(SSD_HEADS_PER_GROUP):
            h = g * SSD_HEADS_PER_GROUP + r
            psl = slice(h * SSD_HEAD_DIM, (h + 1) * SSD_HEAD_DIM)
            rsl = slice(r * SSD_HEAD_DIM, (r + 1) * SSD_HEAD_DIM)
            col = acs[:, h:h + 1]
            row = acs_t[h:h + 1, :]
            last = row[:, tile - 1:tile]
            m_t = (jnp.exp(jnp.minimum(row - col, 0.0)) * cb_t).astype(BF16)
            xh_t = xs_t[psl, :]
            xdt_t = xh_t * dt_t[h:h + 1, :]
            y_diag_t = _dot(xdt_t.astype(BF16), m_t)
            yt_ref[psl, :] = (y_diag_t + y_off_t[rsl, :] * jnp.exp(row)
                              + dskip_ref[h:h + 1, :] * xh_t)
            xd_ref[rsl, :] = (xdt_t * jnp.exp(last - row)).astype(BF16)
            sstate_ref[psl, :] = state[rsl, :] * jnp.exp(last)
            emit_proj()
            next(gla, None)
        sstate_ref[g * hp:(g + 1) * hp, :] += _dot(xd_ref[...], bg)
    emit_proj(len(deferred))
    yz = yt_ref[...].T * _silu(z_ref[...])
    gn = SSD_D_INNER // SSD_GROUPS
    for g in range(SSD_GROUPS):
        csl = slice(g * gn, (g + 1) * gn)
        ys_ref[:, csl] = _rms(yz[:, csl], ssdnorm_ref[:, csl]).astype(BF16)
    merged_ssd = jax.nn.sigmoid(gs_ref[...]) * _dot(ys_ref[...], wus_ref[...])

    for _ in gla:
        pass

    merged = merged_ssd + jax.nn.sigmoid(gg_ref[...]) * _dot(yg_ref[...], wug_ref[...])
    h_ref[...] = x_ref[...] + _dot(merged.astype(BF16), wo_ref[...])


def _memkv_kernel(mem_ref, g_ref, wkv_ref, wq_ref, wo_ref, wqk_ref, vwo_ref):
    mem_len = mem_ref.shape[0]
    m = _rms(mem_ref[...], g_ref[...]).astype(BF16)
    kv = _dot(m, wkv_ref[...])
    for j in range(XATTN_HEADS):
        dsl = slice(j * XATTN_HEAD_DIM, (j + 1) * XATTN_HEAD_DIM)
        msl = slice(j * mem_len, (j + 1) * mem_len)
        k_j = kv[:, dsl].astype(BF16)
        v_j = kv[:, D_MODEL + j * XATTN_HEAD_DIM:D_MODEL + (j + 1) * XATTN_HEAD_DIM].astype(BF16)
        wqk_ref[:, msl] = (_dot_nt(wq_ref[:, dsl], k_j) * (XATTN_HEAD_DIM ** -0.5)).astype(BF16)
        vwo_ref[msl, :] = _dot(v_j, wo_ref[dsl, :]).astype(BF16)


def _tail_kernel(h_ref, gx_ref, wqk_ref, vwo_ref, gf_ref, wfi_ref, wfo_ref, gfin_ref,
                 out_ref, p_ref, act_ref):
    mem_len = wqk_ref.shape[1] // XATTN_HEADS
    ff_block = D_FF // 2
    rows_per_group = h_ref.shape[0] // TAIL_ROW_GROUPS
    groups = [slice(g * rows_per_group, (g + 1) * rows_per_group) for g in range(TAIL_ROW_GROUPS)]
    h1 = [h_ref[rows, :] for rows in groups]
    sc = [_dot(_rms(h, gx_ref[...]).astype(BF16), wqk_ref[...]) for h in h1]
    for rows, s in zip(groups, sc):
        for j in range(XATTN_HEADS):
            msl = slice(j * mem_len, (j + 1) * mem_len)
            e = jnp.exp(s[:, msl] - jnp.max(s[:, msl], axis=-1, keepdims=True))
            p_ref[rows, msl] = (e / jnp.sum(e, axis=-1, keepdims=True)).astype(BF16)
    h2 = [h + _dot(p_ref[rows, :], vwo_ref[...]) for rows, h in zip(groups, h1)]
    n3 = [_rms(h, gf_ref[...]).astype(BF16) for h in h2]
    for cb in range(D_FF // ff_block):
        for rows, n in zip(groups, n3):
            gate = _dot(n, wfi_ref[:, cb * ff_block:(cb + 1) * ff_block])
            up = _dot(n, wfi_ref[:, D_FF + cb * ff_block:D_FF + (cb + 1) * ff_block])
            act_ref[rows, cb * ff_block:(cb + 1) * ff_block] = (_silu(gate) * up).astype(BF16)
    for rows, h in zip(groups, h2):
        h3 = h + _dot(act_ref[rows, :], wfo_ref[...])
        out_ref[rows, :] = _rms(h3, gfin_ref[...])


IN_BOUNDS = tuple(sum(IN_SIZES[:i]) for i in range(len(IN_SIZES) + 1))
IN_RUNS = ((OFF_Z, IN_BOUNDS[0], IN_BOUNDS[2] - IN_BOUNDS[0]),
           (OFF_Q, IN_BOUNDS[3], IN_BOUNDS[7] - IN_BOUNDS[3]),
           (OFF_GS, IN_BOUNDS[8], IN_BOUNDS[10] - IN_BOUNDS[8]))
IN_DT_COL = IN_BOUNDS[2]
IN_A1_COL = IN_BOUNDS[7]
assert all(dst % PREP_SLAB == 0 and width % PREP_SLAB == 0 for dst, _, width in IN_RUNS)
assert OFF_SMALL == (PREP_STEPS - 1) * PREP_SLAB and IN_WIDTH_PADDED == PREP_STEPS * PREP_SLAB


def _prep_src_row(i):
    unit = SSD_HEADS
    start = i * (PREP_SLAB // unit)
    prev_shift = 0
    for dst, src, _ in IN_RUNS:
        start = start + jnp.where(i >= dst // PREP_SLAB, (src - dst - prev_shift) // unit, 0)
        prev_shift = src - dst
    last = (IN_BOUNDS[-1] - PREP_SLAB) // unit
    return jnp.minimum(start, last) * unit


def _prep_kernel(wint_ref, dt_ref, a1_ref, *refs):
    n_plain = (len(refs) - 1) // 2
    plain_in, owin_ref, plain_out = refs[:n_plain], refs[n_plain], refs[n_plain + 1:]
    i = pl.program_id(0)

    @pl.when(i < PREP_STEPS - 1)
    def _():
        owin_ref[...] = wint_ref[...].T.astype(BF16)

    @pl.when(i == PREP_STEPS - 1)
    def _():
        pad = jnp.zeros((LANES - SSD_HEADS - GLA_GATE_RANK, D_MODEL), F32)
        small_t = jnp.concatenate([dt_ref[...], a1_ref[...], pad], axis=0)
        owin_ref[:, 0:LANES] = small_t.T.astype(BF16)
        owin_ref[:, LANES:] = jnp.zeros((D_MODEL, PREP_SLAB - LANES), BF16)

    for src_ref, dst_ref in zip(plain_in, plain_out):
        dst_ref[...] = src_ref[...].astype(BF16)


def _resident(shape):
    return pl.BlockSpec(shape, lambda b, s: (0,) * len(shape), pipeline_mode=pl.Buffered(1))


def _row(v, width=None):
    v = v.reshape(1, -1).astype(F32)
    if width is not None and v.shape[1] < width:
        v = jnp.pad(v, ((0, 0), (0, width - v.shape[1])))
    return v


@functools.partial(jax.jit, static_argnames=("tile", "tail_tile"))
def _forward(x, mem, norm_mix, w_in, ssd_conv_w, ssd_conv_b, ssd_dt_bias, ssd_A_log, ssd_D, ssd_norm,
             gla_w_a2, gla_b_a, gla_norm, w_up_ssd, w_up_gla, w_o, norm_xattn, norm_mem, w_xq, w_xkv,
             w_xo, norm_ffn, w_ffn_in, w_ffn_out, norm_final, *, tile, tail_tile):
    batch, seq, _ = x.shape
    mem_len = mem.shape[1]
    assert seq % tile == 0 and tile % CHUNK == 0
    grid = (batch, seq // tile)

    w_in_t = jnp.swapaxes(w_in[0], 0, 1)
    plain = [w_up_ssd, w_up_gla, w_o, w_xq, w_xkv, w_xo, w_ffn_in, w_ffn_out]
    row_block_in = lambda a: pl.BlockSpec((None, a.shape[1] // PREP_STEPS, a.shape[2]),
                                          lambda i: (0, i, 0))
    row_block_out = lambda a: pl.BlockSpec((a.shape[0] // PREP_STEPS, a.shape[1]), lambda i: (i, 0))
    rows_at = lambda n, start: pl.BlockSpec((pl.Element(n), pl.Element(D_MODEL)),
                                            lambda i: (start(i), 0))
    outs = [jax.ShapeDtypeStruct(a.shape[1:], BF16) for a in plain]
    (w_in_r, w_up_ssd_b, w_up_gla_b, w_o_b, w_xq_b, w_xkv_b, w_xo_b, w_ffn_in_b,
     w_ffn_out_b) = pl.pallas_call(
        _prep_kernel,
        grid=(PREP_STEPS,),
        in_specs=([rows_at(PREP_SLAB, _prep_src_row),
                   rows_at(SSD_HEADS, lambda i: IN_DT_COL), rows_at(GLA_GATE_RANK, lambda i: IN_A1_COL)]
                  + [row_block_in(a) for a in plain]),
        out_specs=([pl.BlockSpec((D_MODEL, PREP_SLAB), lambda i: (0, i))]
                   + [row_block_out(a) for a in outs]),
        out_shape=[jax.ShapeDtypeStruct((D_MODEL, IN_WIDTH_PADDED), BF16)] + outs,
        compiler_params=pltpu.CompilerParams(
            dimension_semantics=("arbitrary",), vmem_limit_bytes=VMEM_LIMIT_BYTES),
        name="prep",
    )(w_in_t, w_in_t, w_in_t, *plain)
    wa2 = jnp.zeros((LANES, GLA_DK), F32).at[SMALL_A1:SMALL_A1 + GLA_GATE_RANK].set(gla_w_a2[0]).astype(BF16)
    dskip = jnp.broadcast_to(ssd_D[0].astype(F32)[:, None], (SSD_HEADS, tile))

    tok_spec = pl.BlockSpec((None, tile, D_MODEL), lambda b, s: (b, s, 0))

    mixer_inputs = [
        (x, tok_spec),
        (_row(norm_mix[0]), _resident((1, D_MODEL))),
        (w_in_r, _resident((D_MODEL, IN_WIDTH_PADDED))),
        (ssd_conv_w[0].reshape(SSD_CONV, SSD_CONV_CH), _resident((SSD_CONV, SSD_CONV_CH))),
        (_row(ssd_conv_b[0]), _resident((1, SSD_CONV_CH))),
        (_row(ssd_dt_bias[0], LANES), _resident((1, LANES))),
        (_row(ssd_A_log[0], LANES), _resident((1, LANES))),
        (dskip, _resident((SSD_HEADS, tile))),
        (_row(ssd_norm[0]), _resident((1, SSD_D_INNER))),
        (wa2, _resident((LANES, GLA_DK))),
        (_row(gla_b_a[0]), _resident((1, GLA_DK))),
        (_row(gla_norm[0]), _resident((1, GLA_HEAD_V))),
        (w_up_ssd_b, _resident((SSD_D_INNER, D_MODEL))),
        (w_up_gla_b, _resident((GLA_DV, D_MODEL))),
        (w_o_b, _resident((D_MODEL, D_MODEL))),
    ]
    mixer_scratch = [
        pltpu.VMEM((tile, D_MODEL), BF16),
        pltpu.VMEM((tile, SSD_D_INNER), F32),
        pltpu.VMEM((tile + CONV_HALO, SSD_CONV_CH), F32),
        pltpu.VMEM((tile, GLA_DK), F32),
        pltpu.VMEM((tile, GLA_DK), F32),
        pltpu.VMEM((tile, GLA_DV), BF16),
        pltpu.VMEM((tile, GLA_DV), F32),
        pltpu.VMEM((tile, D_MODEL), F32),
        pltpu.VMEM((tile, D_MODEL), F32),
        pltpu.VMEM((tile, SSD_D_INNER), F32),
        pltpu.VMEM((tile, SSD_GROUPS * SSD_STATE), BF16),
        pltpu.VMEM((tile, SSD_GROUPS * SSD_STATE), BF16),
        pltpu.VMEM((tile, LANES), F32),
        pltpu.VMEM((tile, LANES), F32),
        pltpu.VMEM((tile, GLA_DK), F32),
        pltpu.VMEM((SSD_D_INNER, tile), F32),
        pltpu.VMEM((SSD_HEADS_PER_GROUP * SSD_HEAD_DIM, tile), BF16),
        pltpu.VMEM((tile, SSD_D_INNER), BF16),
        pltpu.VMEM((tile, GLA_DK), BF16),
        pltpu.VMEM((tile, GLA_DK), BF16),
        pltpu.VMEM((tile, GLA_DK), BF16),
        pltpu.VMEM((tile // CHUNK * GLA_HEADS, CHUNK, CHUNK), BF16),
        pltpu.VMEM((tile // CHUNK, GLA_DK, GLA_HEAD_V), BF16),
        pltpu.VMEM((tile, GLA_DV), F32),
        pltpu.VMEM((tile, GLA_DV), BF16),
        pltpu.VMEM((SSD_D_INNER, SSD_STATE), F32),
        pltpu.VMEM((GLA_DK, GLA_HEAD_V), F32),
    ]
    h1 = pl.pallas_call(
        functools.partial(_mixer_kernel, tile=tile),
        grid=grid,
        in_specs=[spec for _, spec in mixer_inputs],
        out_specs=tok_spec,
        out_shape=jax.ShapeDtypeStruct(x.shape, F32),
        scratch_shapes=mixer_scratch,
        compiler_params=pltpu.CompilerParams(
            dimension_semantics=("arbitrary", "arbitrary"), vmem_limit_bytes=VMEM_LIMIT_BYTES),
        name="mixer",
    )(*[a for a, _ in mixer_inputs])

    att_width = XATTN_HEADS * mem_len
    mem_spec = pl.BlockSpec((None, mem_len, D_MODEL), lambda b: (b, 0, 0))
    whole = lambda shape: pl.BlockSpec(shape, lambda b: (0,) * len(shape))
    wqk, vwo = pl.pallas_call(
        _memkv_kernel,
        grid=(batch,),
        in_specs=[mem_spec, whole((1, D_MODEL)), whole((D_MODEL, 2 * D_MODEL)),
                  whole((D_MODEL, D_MODEL)), whole((D_MODEL, D_MODEL))],
        out_specs=[pl.BlockSpec((None, D_MODEL, att_width), lambda b: (b, 0, 0)),
                   pl.BlockSpec((None, att_width, D_MODEL), lambda b: (b, 0, 0))],
        out_shape=[jax.ShapeDtypeStruct((batch, D_MODEL, att_width), BF16),
                   jax.ShapeDtypeStruct((batch, att_width, D_MODEL), BF16)],
        compiler_params=pltpu.CompilerParams(
            dimension_semantics=("arbitrary",), vmem_limit_bytes=VMEM_LIMIT_BYTES),
        name="memkv",
    )(mem, _row(norm_mem[0]), w_xkv_b, w_xq_b, w_xo_b)

    assert seq % tail_tile == 0
    tail_spec = pl.BlockSpec((None, tail_tile, D_MODEL), lambda b, s: (b, s, 0))
    out = pl.pallas_call(
        _tail_kernel,
        grid=(batch, seq // tail_tile),
        in_specs=[tail_spec,
                  _resident((1, D_MODEL)),
                  pl.BlockSpec((None, D_MODEL, att_width), lambda b, s: (b, 0, 0)),
                  pl.BlockSpec((None, att_width, D_MODEL), lambda b, s: (b, 0, 0)),
                  _resident((1, D_MODEL)),
                  _resident((D_MODEL, 2 * D_FF)),
                  _resident((D_FF, D_MODEL)),
                  _resident((1, D_MODEL))],
        out_specs=tail_spec,
        out_shape=jax.ShapeDtypeStruct(x.shape, F32),
        scratch_shapes=[pltpu.VMEM((tail_tile, att_width), BF16),
                        pltpu.VMEM((tail_tile, D_FF), BF16)],
        compiler_params=pltpu.CompilerParams(
            dimension_semantics=("arbitrary", "arbitrary"), vmem_limit_bytes=VMEM_LIMIT_BYTES),
        name="tail",
    )(h1, _row(norm_xattn[0]), wqk, vwo, _row(norm_ffn[0]), w_ffn_in_b, w_ffn_out_b, _row(norm_final))
    return out


def kernel(x, mem, norm_mix, w_in, ssd_conv_w, ssd_conv_b, ssd_dt_bias, ssd_A_log, ssd_D, ssd_norm,
           gla_w_a2, gla_b_a, gla_norm, w_up_ssd, w_up_gla, w_o, norm_xattn, norm_mem, w_xq, w_xkv,
           w_xo, norm_ffn, w_ffn_in, w_ffn_out, norm_final):
    return _forward(x, mem, norm_mix, w_in, ssd_conv_w, ssd_conv_b, ssd_dt_bias, ssd_A_log, ssd_D,
                    ssd_norm, gla_w_a2, gla_b_a, gla_norm, w_up_ssd, w_up_gla, w_o, norm_xattn,
                    norm_mem, w_xq, w_xkv, w_xo, norm_ffn, w_ffn_in, w_ffn_out, norm_final,
                    tile=MIXER_TILE, tail_tile=TAIL_TILE)
```

```python
import functools

import jax
import jax.numpy as jnp
from jax import lax
from jax.experimental import pallas as pl
from jax.experimental.pallas import tpu as pltpu

F32 = jnp.float32
BF16 = jnp.bfloat16

D_MODEL = 1024
EPS = 1e-6
CHUNK = 64
SSD_D_INNER = 1024
SSD_HEAD_DIM = 64
SSD_HEADS = 16
SSD_GROUPS = 2
SSD_HEADS_PER_GROUP = SSD_HEADS // SSD_GROUPS
SSD_STATE = 128
SSD_CONV = 4
SSD_CONV_CH = SSD_D_INNER + 2 * SSD_GROUPS * SSD_STATE
GLA_HEADS = 4
GLA_DK = 512
GLA_DV = 1024
GLA_HEAD_K = GLA_DK // GLA_HEADS
GLA_HEAD_V = GLA_DV // GLA_HEADS
GLA_GATE_RANK = 16
GLA_TAU = 16.0
XATTN_HEADS = 4
XATTN_HEAD_DIM = D_MODEL // XATTN_HEADS
D_FF = 2816
IN_SIZES = (SSD_D_INNER, SSD_CONV_CH, SSD_HEADS, GLA_DK, GLA_DK, GLA_DV, GLA_DV, GLA_GATE_RANK,
            D_MODEL, D_MODEL)

LANES = 128
SUBLANES = 8
VMEM_LIMIT_BYTES = 56 * 1024 * 1024
MIXER_TILE = 256
TAIL_TILE = 1024
TAIL_ROW_GROUPS = 4
PREP_STEPS = 16
PREP_SLAB = 512

OFF_Z = 0
OFF_XBC = OFF_Z + SSD_D_INNER
OFF_Q = OFF_XBC + SSD_CONV_CH
OFF_K = OFF_Q + GLA_DK
OFF_V = OFF_K + GLA_DK
OFF_R = OFF_V + GLA_DV
OFF_GS = OFF_R + GLA_DV
OFF_GG = OFF_GS + D_MODEL
OFF_SMALL = OFF_GG + D_MODEL
IN_WIDTH_PADDED = OFF_SMALL + PREP_SLAB
IN_WIDTH_ALLOC = IN_WIDTH_PADDED + PREP_SLAB
SMALL_DT = 0
SMALL_A1 = SSD_HEADS

CONV_HALO = SUBLANES


def _pitch(width):
    return width + LANES if (width // LANES) % SUBLANES == 0 else width


def _dot(a, b):
    return jnp.dot(a, b, preferred_element_type=F32)


def _dot_nt(a, b):
    return lax.dot_general(a, b, (((1,), (1,)), ((), ())), preferred_element_type=F32)


def _dot_tn(a, b):
    return lax.dot_general(a, b, (((0,), (0,)), ((), ())), preferred_element_type=F32)


def _rms(x, g):
    return x * lax.rsqrt(jnp.mean(x * x, axis=-1, keepdims=True) + EPS) * g


def _silu(x):
    return x * jax.nn.sigmoid(x)


def _split3(x):
    hi = x.astype(BF16)
    r1 = x - hi.astype(F32)
    mid = r1.astype(BF16)
    lo = (r1 - mid.astype(F32)).astype(BF16)
    return hi, mid, lo


def _cumsum_rows(tri, x):
    hi, mid, lo = _split3(x)
    return _dot(tri, hi) + _dot(tri, mid) + _dot(tri, lo)


def _cumsum_lanes(x, upper):
    hi, mid, lo = _split3(x)
    return _dot(hi, upper) + _dot(mid, upper) + _dot(lo, upper)


def _mixer_kernel(x_ref, gmix_ref, win_ref, convw_ref, convb_ref, dtb_ref, alog_ref, dskip_ref,
                  ssdnorm_ref, wa2_ref, ba_ref, glanorm_ref, wus_ref, wug_ref, wo_ref,
                  h_ref,
                  z_ref, xpad_ref, q_ref, k_ref, v_ref, r_ref, gs_ref, gg_ref,
                  xs_ref, b_ref, c_ref, dt_ref, a_ref, la_ref, yt_ref, xd_ref, ys_ref,
                  qt_ref, kt_ref, kh_ref, att_ref, sb_ref, o_ref, yg_ref,
                  sstate_ref, gstate_ref, *, tile):
    s = pl.program_id(1)

    @pl.when(s == 0)
    def _():
        sstate_ref[...] = jnp.zeros_like(sstate_ref)
        gstate_ref[...] = jnp.zeros_like(gstate_ref)
        xpad_ref[0:CONV_HALO, :] = jnp.zeros((CONV_HALO, SSD_CONV_CH), F32)

    n = _rms(x_ref[...], gmix_ref[...]).astype(BF16)

    def proj(off, width):
        return _dot(n, win_ref[:, off:off + width])

    slab = 512
    deferred = [(dst, off, c0)
                for dst, off, width in ((q_ref, OFF_Q, GLA_DK), (k_ref, OFF_K, GLA_DK),
                                        (v_ref, OFF_V, GLA_DV), (r_ref, OFF_R, GLA_DV),
                                        (z_ref, OFF_Z, SSD_D_INNER), (gs_ref, OFF_GS, D_MODEL),
                                        (gg_ref, OFF_GG, D_MODEL))
                for c0 in range(0, width, slab)]

    def emit_proj(count=1):
        for _ in range(count):
            if deferred:
                dst, off, c0 = deferred.pop(0)
                dst[:, c0:c0 + slab] = proj(off + c0, slab).astype(dst.dtype)

    xpad_ref[CONV_HALO:CONV_HALO + tile, :] = proj(OFF_XBC, SSD_CONV_CH)
    small = proj(OFF_SMALL, LANES)

    dt = jax.nn.softplus(small + dtb_ref[...])
    dt_ref[...] = dt
    a_ref[...] = dt * (-jnp.exp(alog_ref[...]))
    logits = _dot(small.astype(BF16), wa2_ref[...]) + ba_ref[...]
    la_ref[...] = jax.nn.log_sigmoid(logits) / GLA_TAU

    col_block = SSD_GROUPS * SSD_STATE
    for cb in range(SSD_CONV_CH // col_block):
        cols = slice(cb * col_block, (cb + 1) * col_block)
        acc = convb_ref[:, cols]
        for j in range(SSD_CONV):
            lo = CONV_HALO - (SSD_CONV - 1) + j
            acc = acc + convw_ref[j:j + 1, cols] * xpad_ref[lo:lo + tile, cols]
        act = _silu(acc)
        if cb < SSD_D_INNER // col_block:
            xs_ref[:, cols] = act
        elif cb == SSD_D_INNER // col_block:
            b_ref[...] = act.astype(BF16)
        else:
            c_ref[...] = act.astype(BF16)
        emit_proj()
    xpad_ref[0:CONV_HALO, :] = xpad_ref[tile:tile + CONV_HALO, :]

    rid = lax.broadcasted_iota(jnp.int32, (tile, tile), 0)
    cid = lax.broadcasted_iota(jnp.int32, (tile, tile), 1)

    def gla_stages():
        n_chunks = tile // CHUNK
        chunk_of = lambda idx: lax.shift_right_logical(idx, CHUNK.bit_length() - 1)
        tri_blocks = ((rid >= cid) & (chunk_of(rid) == chunk_of(cid))).astype(BF16)
        bcum = _cumsum_rows(tri_blocks, la_ref[...])
        blast = jnp.concatenate(
            [jnp.broadcast_to(bcum[(i + 1) * CHUNK - 1:(i + 1) * CHUNK, :], (CHUNK, GLA_DK))
             for i in range(n_chunks)], axis=0)
        kk = k_ref[...]
        yield
        qt_ref[...] = (q_ref[...] * (GLA_HEAD_K ** -0.5) * jnp.exp(bcum)).astype(BF16)
        yield
        kt_ref[...] = (kk * jnp.exp(-bcum)).astype(BF16)
        yield
        kh_ref[...] = (kk * jnp.exp(blast - bcum)).astype(BF16)
        dec_t = jnp.exp(blast).T
        causal = (lax.broadcasted_iota(jnp.int32, (CHUNK, CHUNK), 0)
                  >= lax.broadcasted_iota(jnp.int32, (CHUNK, CHUNK), 1))
        ksls = [slice(j * GLA_HEAD_K, (j + 1) * GLA_HEAD_K) for j in range(GLA_HEADS)]
        vsls = [slice(j * GLA_HEAD_V, (j + 1) * GLA_HEAD_V) for j in range(GLA_HEADS)]
        for i in range(n_chunks):
            rs = slice(i * CHUNK, (i + 1) * CHUNK)
            for j in range(GLA_HEADS):
                scores = _dot_nt(qt_ref[rs, ksls[j]], kt_ref[rs, ksls[j]])
                att_ref[i * GLA_HEADS + j] = jnp.where(causal, scores, 0.0).astype(BF16)
            yield
        states = [gstate_ref[ksl, :] for ksl in ksls]
        for i in range(n_chunks):
            rs = slice(i * CHUNK, (i + 1) * CHUNK)
            for j in range(GLA_HEADS):
                sb_ref[i, ksls[j], :] = states[j].astype(BF16)
                update = _dot_tn(kh_ref[rs, ksls[j]], v_ref[rs, vsls[j]])
                states[j] = states[j] * dec_t[ksls[j], i * CHUNK:i * CHUNK + 1] + update
            yield
        for j in range(GLA_HEADS):
            gstate_ref[ksls[j], :] = states[j]
        for i in range(n_chunks):
            rs = slice(i * CHUNK, (i + 1) * CHUNK)
            for j in range(GLA_HEADS):
                o_ref[rs, vsls[j]] = (_dot(att_ref[i * GLA_HEADS + j], v_ref[rs, vsls[j]])
                                      + _dot(qt_ref[rs, ksls[j]], sb_ref[i, ksls[j], :]))
            yield
        for j in range(GLA_HEADS):
            yg_ref[:, vsls[j]] = (_rms(o_ref[:, vsls[j]], glanorm_ref[...])
                                  * _silu(r_ref[:, vsls[j]])).astype(BF16)
            yield

    gla = gla_stages()

    upper = rid <= cid
    a_t = a_ref[...].T
    acs_t = _cumsum_lanes(a_t, upper.astype(BF16))
    acs = acs_t.T
    dt_t = dt_ref[...].T
    xs_t = xs_ref[...].T
    hp = SSD_HEADS_PER_GROUP * SSD_HEAD_DIM
    for g in range(SSD_GROUPS):
        nsl = slice(g * SSD_STATE, (g + 1) * SSD_STATE)
        bg = b_ref[:, nsl]
        cg = c_ref[:, nsl]
        cb_t = jnp.where(upper, _dot_nt(bg, cg), 0.0)
        state = sstate_ref[g * hp:(g + 1) * hp, :]
        y_off_t = _dot_nt(state.astype(BF16), cg)
        for r in range(SSD_HEADS_PER_GROUP):
            h = g * SSD_HEADS_PER_GROUP + r
            psl = slice(h * SSD_HEAD_DIM, (h + 1) * SSD_HEAD_DIM)
            rsl = slice(r * SSD_HEAD_DIM, (r + 1) * SSD_HEAD_DIM)
            col = acs[:, h:h + 1]
            row = acs_t[h:h + 1, :]
            last = row[:, tile - 1:tile]
            m_t = (jnp.exp(jnp.minimum(row - col, 0.0)) * cb_t).astype(BF16)
            xh_t = xs_t[psl, :]
            xdt_t = xh_t * dt_t[h:h + 1, :]
            y_diag_t = _dot(xdt_t.astype(BF16), m_t)
            yt_ref[psl, :] = (y_diag_t + y_off_t[rsl, :] * jnp.exp(row)
                              + dskip_ref[h:h + 1, :] * xh_t)
            xd_ref[rsl, :] = (xdt_t * jnp.exp(last - row)).astype(BF16)
            sstate_ref[psl, :] = state[rsl, :] * jnp.exp(last)
            emit_proj()
            next(gla, None)
        sstate_ref[g * hp:(g + 1) * hp, :] += _dot(xd_ref[...], bg)
    emit_proj(len(deferred))
    yz = yt_ref[...].T * _silu(z_ref[...])
    gn = SSD_D_INNER // SSD_GROUPS
    for g in range(SSD_GROUPS):
        csl = slice(g * gn, (g + 1) * gn)
        ys_ref[:, csl] = _rms(yz[:, csl], ssdnorm_ref[:, csl]).astype(BF16)
    merged_ssd = jax.nn.sigmoid(gs_ref[...]) * _dot(ys_ref[...], wus_ref[:, :D_MODEL])

    for _ in gla:
        pass

    merged = merged_ssd + jax.nn.sigmoid(gg_ref[...]) * _dot(yg_ref[...], wug_ref[:, :D_MODEL])
    h_ref[...] = x_ref[...] + _dot(merged.astype(BF16), wo_ref[:, :D_MODEL])


def _memkv_kernel(mem_ref, g_ref, wkv_ref, wq_ref, wo_ref, wqk_ref, vwo_ref):
    mem_len = mem_ref.shape[0]
    m = _rms(mem_ref[...], g_ref[...]).astype(BF16)
    kv = _dot(m, wkv_ref[...])
    for j in range(XATTN_HEADS):
        dsl = slice(j * XATTN_HEAD_DIM, (j + 1) * XATTN_HEAD_DIM)
        msl = slice(j * mem_len, (j + 1) * mem_len)
        k_j = kv[:, dsl].astype(BF16)
        v_j = kv[:, D_MODEL + j * XATTN_HEAD_DIM:D_MODEL + (j + 1) * XATTN_HEAD_DIM].astype(BF16)
        wqk_ref[:, msl] = (_dot_nt(wq_ref[:, dsl], k_j) * (XATTN_HEAD_DIM ** -0.5)).astype(BF16)
        vwo_ref[msl, 0:D_MODEL] = _dot(v_j, wo_ref[dsl, :]).astype(BF16)
    att_width = XATTN_HEADS * mem_len
    for ref, used in ((wqk_ref, att_width), (vwo_ref, D_MODEL)):
        if ref.shape[1] > used:
            ref[:, used:] = jnp.zeros((ref.shape[0], ref.shape[1] - used), BF16)


def _tail_kernel(h_ref, gx_ref, wqk_ref, vwo_ref, gf_ref, wfi_ref, wfo_ref, gfin_ref,
                 out_ref, p_ref, act_ref):
    att_width = p_ref.shape[1]
    mem_len = att_width // XATTN_HEADS
    ff_block = D_FF // 2
    rows_per_group = h_ref.shape[0] // TAIL_ROW_GROUPS
    groups = [slice(g * rows_per_group, (g + 1) * rows_per_group) for g in range(TAIL_ROW_GROUPS)]
    h1 = [h_ref[rows, :] for rows in groups]
    sc = [_dot(_rms(h, gx_ref[...]).astype(BF16), wqk_ref[:, :att_width]) for h in h1]
    for rows, s in zip(groups, sc):
        for j in range(XATTN_HEADS):
            msl = slice(j * mem_len, (j + 1) * mem_len)
            e = jnp.exp(s[:, msl] - jnp.max(s[:, msl], axis=-1, keepdims=True))
            p_ref[rows, msl] = (e / jnp.sum(e, axis=-1, keepdims=True)).astype(BF16)
    h2 = [h + _dot(p_ref[rows, :], vwo_ref[:, :D_MODEL]) for rows, h in zip(groups, h1)]
    n3 = [_rms(h, gf_ref[...]).astype(BF16) for h in h2]
    for cb in range(D_FF // ff_block):
        for rows, n in zip(groups, n3):
            gate = _dot(n, wfi_ref[:, cb * ff_block:(cb + 1) * ff_block])
            up = _dot(n, wfi_ref[:, D_FF + cb * ff_block:D_FF + (cb + 1) * ff_block])
            act_ref[rows, cb * ff_block:(cb + 1) * ff_block] = (_silu(gate) * up).astype(BF16)
    for rows, h in zip(groups, h2):
        h3 = h + _dot(act_ref[rows, :], wfo_ref[:, :D_MODEL])
        out_ref[rows, :] = _rms(h3, gfin_ref[...])


IN_BOUNDS = tuple(sum(IN_SIZES[:i]) for i in range(len(IN_SIZES) + 1))
IN_RUNS = ((OFF_Z, IN_BOUNDS[0], IN_BOUNDS[2] - IN_BOUNDS[0]),
           (OFF_Q, IN_BOUNDS[3], IN_BOUNDS[7] - IN_BOUNDS[3]),
           (OFF_GS, IN_BOUNDS[8], IN_BOUNDS[10] - IN_BOUNDS[8]))
IN_DT_COL = IN_BOUNDS[2]
IN_A1_COL = IN_BOUNDS[7]
assert all(dst % PREP_SLAB == 0 and width % PREP_SLAB == 0 for dst, _, width in IN_RUNS)
assert OFF_SMALL == (PREP_STEPS - 1) * PREP_SLAB and IN_WIDTH_PADDED == PREP_STEPS * PREP_SLAB
assert _pitch(IN_WIDTH_ALLOC) == IN_WIDTH_ALLOC


def _prep_src_row(i):
    unit = SSD_HEADS
    start = i * (PREP_SLAB // unit)
    prev_shift = 0
    for dst, src, _ in IN_RUNS:
        start = start + jnp.where(i >= dst // PREP_SLAB, (src - dst - prev_shift) // unit, 0)
        prev_shift = src - dst
    last = (IN_BOUNDS[-1] - PREP_SLAB) // unit
    return jnp.minimum(start, last) * unit


def _prep_kernel(wint_ref, dt_ref, a1_ref, *refs):
    n_plain = (len(refs) - 1) // 2
    plain_in, owin_ref, plain_out = refs[:n_plain], refs[n_plain], refs[n_plain + 1:]
    i = pl.program_id(0)

    @pl.when(i < PREP_STEPS - 1)
    def _():
        owin_ref[...] = wint_ref[...].T.astype(BF16)

    @pl.when(i == PREP_STEPS - 1)
    def _():
        pad = jnp.zeros((LANES - SSD_HEADS - GLA_GATE_RANK, D_MODEL), F32)
        small_t = jnp.concatenate([dt_ref[...], a1_ref[...], pad], axis=0)
        owin_ref[:, 0:LANES] = small_t.T.astype(BF16)
        owin_ref[:, LANES:] = jnp.zeros((D_MODEL, PREP_SLAB - LANES), BF16)

    @pl.when(i == PREP_STEPS)
    def _():
        owin_ref[...] = jnp.zeros(owin_ref.shape, BF16)

    for src_ref, dst_ref in zip(plain_in, plain_out):
        cols = src_ref.shape[1]
        dst_ref[:, 0:cols] = src_ref[...].astype(BF16)
        if dst_ref.shape[1] > cols:
            dst_ref[:, cols:] = jnp.zeros((dst_ref.shape[0], dst_ref.shape[1] - cols), BF16)


def _resident(shape):
    return pl.BlockSpec(shape, lambda b, s: (0,) * len(shape), pipeline_mode=pl.Buffered(1))


def _row(v, width=None):
    v = v.reshape(1, -1).astype(F32)
    if width is not None and v.shape[1] < width:
        v = jnp.pad(v, ((0, 0), (0, width - v.shape[1])))
    return v


@functools.partial(jax.jit, static_argnames=("tile", "tail_tile"))
def _forward(x, mem, norm_mix, w_in, ssd_conv_w, ssd_conv_b, ssd_dt_bias, ssd_A_log, ssd_D, ssd_norm,
             gla_w_a2, gla_b_a, gla_norm, w_up_ssd, w_up_gla, w_o, norm_xattn, norm_mem, w_xq, w_xkv,
             w_xo, norm_ffn, w_ffn_in, w_ffn_out, norm_final, *, tile, tail_tile):
    batch, seq, _ = x.shape
    mem_len = mem.shape[1]
    assert seq % tile == 0 and tile % CHUNK == 0
    grid = (batch, seq // tile)

    w_in_t = jnp.swapaxes(w_in[0], 0, 1)
    plain = [w_up_ssd, w_up_gla, w_o, w_xq, w_xkv, w_xo, w_ffn_in, w_ffn_out]
    row_of = lambda i: jnp.minimum(i, PREP_STEPS - 1)
    row_block_in = lambda a: pl.BlockSpec((None, a.shape[1] // PREP_STEPS, a.shape[2]),
                                          lambda i: (0, row_of(i), 0))
    row_block_out = lambda a: pl.BlockSpec((a.shape[0] // PREP_STEPS, a.shape[1]),
                                           lambda i: (row_of(i), 0))
    rows_at = lambda n, start: pl.BlockSpec((pl.Element(n), pl.Element(D_MODEL)),
                                            lambda i: (start(i), 0))
    streamed = (w_up_ssd, w_up_gla, w_o, w_ffn_in, w_ffn_out)
    outs = [jax.ShapeDtypeStruct((a.shape[1], _pitch(a.shape[2]) if any(a is s for s in streamed)
                                  else a.shape[2]), BF16) for a in plain]
    (w_in_r, w_up_ssd_b, w_up_gla_b, w_o_b, w_xq_b, w_xkv_b, w_xo_b, w_ffn_in_b,
     w_ffn_out_b) = pl.pallas_call(
        _prep_kernel,
        grid=(PREP_STEPS + 1,),
        in_specs=([rows_at(PREP_SLAB, _prep_src_row),
                   rows_at(SSD_HEADS, lambda i: IN_DT_COL), rows_at(GLA_GATE_RANK, lambda i: IN_A1_COL)]
                  + [row_block_in(a) for a in plain]),
        out_specs=([pl.BlockSpec((D_MODEL, PREP_SLAB), lambda i: (0, i))]
                   + [row_block_out(a) for a in outs]),
        out_shape=[jax.ShapeDtypeStruct((D_MODEL, IN_WIDTH_ALLOC), BF16)] + outs,
        compiler_params=pltpu.CompilerParams(
            dimension_semantics=("arbitrary",), vmem_limit_bytes=VMEM_LIMIT_BYTES),
        name="prep",
    )(w_in_t, w_in_t, w_in_t, *plain)
    wa2 = jnp.zeros((LANES, GLA_DK), F32).at[SMALL_A1:SMALL_A1 + GLA_GATE_RANK].set(gla_w_a2[0]).astype(BF16)
    dskip = jnp.broadcast_to(ssd_D[0].astype(F32)[:, None], (SSD_HEADS, tile))

    tok_spec = pl.BlockSpec((None, tile, D_MODEL), lambda b, s: (b, s, 0))

    mixer_inputs = [
        (x, tok_spec),
        (_row(norm_mix[0]), _resident((1, D_MODEL))),
        (w_in_r, _resident((D_MODEL, IN_WIDTH_ALLOC))),
        (ssd_conv_w[0].reshape(SSD_CONV, SSD_CONV_CH), _resident((SSD_CONV, SSD_CONV_CH))),
        (_row(ssd_conv_b[0]), _resident((1, SSD_CONV_CH))),
        (_row(ssd_dt_bias[0], LANES), _resident((1, LANES))),
        (_row(ssd_A_log[0], LANES), _resident((1, LANES))),
        (dskip, _resident((SSD_HEADS, tile))),
        (_row(ssd_norm[0]), _resident((1, SSD_D_INNER))),
        (wa2, _resident((LANES, GLA_DK))),
        (_row(gla_b_a[0]), _resident((1, GLA_DK))),
        (_row(gla_norm[0]), _resident((1, GLA_HEAD_V))),
        (w_up_ssd_b, _resident(w_up_ssd_b.shape)),
        (w_up_gla_b, _resident(w_up_gla_b.shape)),
        (w_o_b, _resident(w_o_b.shape)),
    ]
    mixer_scratch = [
        pltpu.VMEM((tile, SSD_D_INNER), F32),
        pltpu.VMEM((tile + CONV_HALO, SSD_CONV_CH), F32),
        pltpu.VMEM((tile, GLA_DK), F32),
        pltpu.VMEM((tile, GLA_DK), F32),
        pltpu.VMEM((tile, GLA_DV), BF16),
        pltpu.VMEM((tile, GLA_DV), F32),
        pltpu.VMEM((tile, D_MODEL), F32),
        pltpu.VMEM((tile, D_MODEL), F32),
        pltpu.VMEM((tile, SSD_D_INNER), F32),
        pltpu.VMEM((tile, SSD_GROUPS * SSD_STATE), BF16),
        pltpu.VMEM((tile, SSD_GROUPS * SSD_STATE), BF16),
        pltpu.VMEM((tile, LANES), F32),
        pltpu.VMEM((tile, LANES), F32),
        pltpu.VMEM((tile, GLA_DK), F32),
        pltpu.VMEM((SSD_D_INNER, tile), F32),
        pltpu.VMEM((SSD_HEADS_PER_GROUP * SSD_HEAD_DIM, tile), BF16),
        pltpu.VMEM((tile, SSD_D_INNER), BF16),
        pltpu.VMEM((tile, GLA_DK), BF16),
        pltpu.VMEM((tile, GLA_DK), BF16),
        pltpu.VMEM((tile, GLA_DK), BF16),
        pltpu.VMEM((tile // CHUNK * GLA_HEADS, CHUNK, CHUNK), BF16),
        pltpu.VMEM((tile // CHUNK, GLA_DK, GLA_HEAD_V), BF16),
        pltpu.VMEM((tile, GLA_DV), F32),
        pltpu.VMEM((tile, GLA_DV), BF16),
        pltpu.VMEM((SSD_D_INNER, SSD_STATE), F32),
        pltpu.VMEM((GLA_DK, GLA_HEAD_V), F32),
    ]
    h1 = pl.pallas_call(
        functools.partial(_mixer_kernel, tile=tile),
        grid=grid,
        in_specs=[spec for _, spec in mixer_inputs],
        out_specs=tok_spec,
        out_shape=jax.ShapeDtypeStruct(x.shape, F32),
        scratch_shapes=mixer_scratch,
        compiler_params=pltpu.CompilerParams(
            dimension_semantics=("arbitrary", "arbitrary"), vmem_limit_bytes=VMEM_LIMIT_BYTES),
        name="mixer",
    )(*[a for a, _ in mixer_inputs])

    att_width = XATTN_HEADS * mem_len
    wqk_shape = (D_MODEL, _pitch(att_width))
    vwo_shape = (att_width, _pitch(D_MODEL))
    mem_spec = pl.BlockSpec((None, mem_len, D_MODEL), lambda b: (b, 0, 0))
    whole = lambda shape: pl.BlockSpec(shape, lambda b: (0,) * len(shape))
    wqk, vwo = pl.pallas_call(
        _memkv_kernel,
        grid=(batch,),
        in_specs=[mem_spec, whole((1, D_MODEL)), whole((D_MODEL, 2 * D_MODEL)),
                  whole((D_MODEL, D_MODEL)), whole((D_MODEL, D_MODEL))],
        out_specs=[pl.BlockSpec((None,) + wqk_shape, lambda b: (b, 0, 0)),
                   pl.BlockSpec((None,) + vwo_shape, lambda b: (b, 0, 0))],
        out_shape=[jax.ShapeDtypeStruct((batch,) + wqk_shape, BF16),
                   jax.ShapeDtypeStruct((batch,) + vwo_shape, BF16)],
        compiler_params=pltpu.CompilerParams(
            dimension_semantics=("arbitrary",), vmem_limit_bytes=VMEM_LIMIT_BYTES),
        name="memkv",
    )(mem, _row(norm_mem[0]), w_xkv_b, w_xq_b, w_xo_b)

    assert seq % tail_tile == 0
    tail_spec = pl.BlockSpec((None, tail_tile, D_MODEL), lambda b, s: (b, s, 0))
    out = pl.pallas_call(
        _tail_kernel,
        grid=(batch, seq // tail_tile),
        in_specs=[tail_spec,
                  _resident((1, D_MODEL)),
                  pl.BlockSpec((None,) + wqk_shape, lambda b, s: (b, 0, 0)),
                  pl.BlockSpec((None,) + vwo_shape, lambda b, s: (b, 0, 0)),
                  _resident((1, D_MODEL)),
                  _resident(w_ffn_in_b.shape),
                  _resident(w_ffn_out_b.shape),
                  _resident((1, D_MODEL))],
        out_specs=tail_spec,
        out_shape=jax.ShapeDtypeStruct(x.shape, F32),
        scratch_shapes=[pltpu.VMEM((tail_tile, att_width), BF16),
                        pltpu.VMEM((tail_tile, D_FF), BF16)],
        compiler_params=pltpu.CompilerParams(
            dimension_semantics=("arbitrary", "arbitrary"), vmem_limit_bytes=VMEM_LIMIT_BYTES),
        name="tail",
    )(h1, _row(norm_xattn[0]), wqk, vwo, _row(norm_ffn[0]), w_ffn_in_b, w_ffn_out_b, _row(norm_final))
    return out


def kernel(x, mem, norm_mix, w_in, ssd_conv_w, ssd_conv_b, ssd_dt_bias, ssd_A_log, ssd_D, ssd_norm,
           gla_w_a2, gla_b_a, gla_norm, w_up_ssd, w_up_gla, w_o, norm_xattn, norm_mem, w_xq, w_xkv,
           w_xo, norm_ffn, w_ffn_in, w_ffn_out, norm_final):
    return _forward(x, mem, norm_mix, w_in, ssd_conv_w, ssd_conv_b, ssd_dt_bias, ssd_A_log, ssd_D,
                    ssd_norm, gla_w_a2, gla_b_a, gla_norm, w_up_ssd, w_up_gla, w_o, norm_xattn,
                    norm_mem, w_xq, w_xkv, w_xo, norm_ffn, w_ffn_in, w_ffn_out, norm_final,
                    tile=MIXER_TILE, tail_tile=TAIL_TILE)
```

```python
import functools

import jax
import jax.numpy as jnp
from jax import lax
from jax.experimental import pallas as pl
from jax.experimental.pallas import tpu as pltpu

F32 = jnp.float32
BF16 = jnp.bfloat16

D_MODEL = 1024
EPS = 1e-6
CHUNK = 64
SSD_D_INNER = 1024
SSD_HEAD_DIM = 64
SSD_HEADS = 16
SSD_GROUPS = 2
SSD_HEADS_PER_GROUP = SSD_HEADS // SSD_GROUPS
SSD_STATE = 128
SSD_CONV = 4
SSD_CONV_CH = SSD_D_INNER + 2 * SSD_GROUPS * SSD_STATE
GLA_HEADS = 4
GLA_DK = 512
GLA_DV = 1024
GLA_HEAD_K = GLA_DK // GLA_HEADS
GLA_HEAD_V = GLA_DV // GLA_HEADS
GLA_GATE_RANK = 16
GLA_TAU = 16.0
XATTN_HEADS = 4
XATTN_HEAD_DIM = D_MODEL // XATTN_HEADS
D_FF = 2816
IN_SIZES = (SSD_D_INNER, SSD_CONV_CH, SSD_HEADS, GLA_DK, GLA_DK, GLA_DV, GLA_DV, GLA_GATE_RANK,
            D_MODEL, D_MODEL)

LANES = 128
SUBLANES = 8
VMEM_LIMIT_BYTES = 56 * 1024 * 1024
MIXER_TILE = 256
TAIL_TILE = 1024
TAIL_ROW_GROUPS = 4
PREP_STEPS = 16
PREP_SLAB = 512

OFF_Z = 0
OFF_XBC = OFF_Z + SSD_D_INNER
OFF_Q = OFF_XBC + SSD_CONV_CH
OFF_K = OFF_Q + GLA_DK
OFF_V = OFF_K + GLA_DK
OFF_R = OFF_V + GLA_DV
OFF_GS = OFF_R + GLA_DV
OFF_GG = OFF_GS + D_MODEL
OFF_SMALL = OFF_GG + D_MODEL
IN_WIDTH_PADDED = OFF_SMALL + PREP_SLAB
IN_WIDTH_ALLOC = IN_WIDTH_PADDED + PREP_SLAB
SMALL_DT = 0
SMALL_A1 = SSD_HEADS

CONV_HALO = SUBLANES
CONV_SLABS = SSD_CONV_CH // LANES
CONV_PHASES = 4
assert SSD_STATE == LANES and LANES % SSD_HEAD_DIM == 0


def _pitch(width):
    return width + LANES if (width // LANES) % SUBLANES == 0 else width


def _dot(a, b):
    return jnp.dot(a, b, preferred_element_type=F32)


def _dot_nt(a, b):
    return lax.dot_general(a, b, (((1,), (1,)), ((), ())), preferred_element_type=F32)


def _dot_tn(a, b):
    return lax.dot_general(a, b, (((0,), (0,)), ((), ())), preferred_element_type=F32)


def _rms(x, g):
    return x * lax.rsqrt(jnp.mean(x * x, axis=-1, keepdims=True) + EPS) * g


def _silu(x):
    return x * jax.nn.sigmoid(x)


def _split3(x):
    hi = x.astype(BF16)
    r1 = x - hi.astype(F32)
    mid = r1.astype(BF16)
    lo = (r1 - mid.astype(F32)).astype(BF16)
    return hi, mid, lo


def _cumsum_rows(tri, x):
    hi, mid, lo = _split3(x)
    return _dot(tri, hi) + _dot(tri, mid) + _dot(tri, lo)


def _cumsum_lanes(x, upper):
    hi, mid, lo = _split3(x)
    return _dot(hi, upper) + _dot(mid, upper) + _dot(lo, upper)


def _mixer_kernel(x_ref, gmix_ref, win_ref, convw_ref, convb_ref, dtb_ref, alog_ref, dskip_ref,
                  ssdnorm_ref, wa2_ref, ba_ref, glanorm_ref, wus_ref, wug_ref, wo_ref,
                  h_ref,
                  z_ref, xpad_ref, q_ref, k_ref, v_ref, r_ref, gs_ref, gg_ref,
                  xbc_ref, b_ref, c_ref, dt_ref, a_ref, la_ref, yt_ref, xd_ref, ys_ref,
                  qt_ref, kt_ref, kh_ref, att_ref, sb_ref, o_ref, yg_ref,
                  sstate_ref, gstate_ref, *, tile):
    s = pl.program_id(1)

    @pl.when(s == 0)
    def _():
        sstate_ref[...] = jnp.zeros_like(sstate_ref)
        gstate_ref[...] = jnp.zeros_like(gstate_ref)
        xpad_ref[:, 0:CONV_HALO, :] = jnp.zeros((CONV_SLABS, CONV_HALO, LANES), F32)

    n = _rms(x_ref[...], gmix_ref[...]).astype(BF16)

    def proj(off, width):
        return _dot(n, win_ref[:, off:off + width])

    slab = 512
    deferred = [(dst, off, c0)
                for dst, off, width in ((q_ref, OFF_Q, GLA_DK), (k_ref, OFF_K, GLA_DK),
                                        (v_ref, OFF_V, GLA_DV), (r_ref, OFF_R, GLA_DV),
                                        (z_ref, OFF_Z, SSD_D_INNER), (gs_ref, OFF_GS, D_MODEL),
                                        (gg_ref, OFF_GG, D_MODEL))
                for c0 in range(0, width, slab)]

    def emit_proj(count=1):
        for _ in range(count):
            if deferred:
                dst, off, c0 = deferred.pop(0)
                dst[:, c0:c0 + slab] = proj(off + c0, slab).astype(dst.dtype)

    for c0 in range(0, SSD_CONV_CH, slab):
        xbc = proj(OFF_XBC + c0, slab)
        for t in range(slab // LANES):
            xpad_ref[c0 // LANES + t, CONV_HALO:CONV_HALO + tile, :] = xbc[:, t * LANES:(t + 1) * LANES]
    small = proj(OFF_SMALL, LANES)

    dt = jax.nn.softplus(small + dtb_ref[...])
    dt_ref[...] = dt
    a_ref[...] = dt * (-jnp.exp(alog_ref[...]))
    logits = _dot(small.astype(BF16), wa2_ref[...]) + ba_ref[...]
    la_ref[...] = jax.nn.log_sigmoid(logits) / GLA_TAU

    rows_per_phase = tile // CONV_PHASES
    first = CONV_HALO - (SSD_CONV - 1)
    for sl in range(CONV_SLABS):
        cols = slice(sl * LANES, (sl + 1) * LANES)
        taps = {d: xpad_ref[sl, pl.ds(first + d, rows_per_phase, stride=CONV_PHASES), :]
                for d in range(CONV_PHASES + SSD_CONV - 1)}
        for phase in range(CONV_PHASES):
            acc = convb_ref[:, cols]
            for j in range(SSD_CONV):
                acc = acc + convw_ref[j:j + 1, cols] * taps[phase + j]
            xbc_ref[sl, pl.ds(phase, rows_per_phase, stride=CONV_PHASES), :] = _silu(acc)
        if sl % 2 == 1:
            emit_proj()
    xpad_ref[:, 0:CONV_HALO, :] = xpad_ref[:, tile:tile + CONV_HALO, :]
    x_slabs = SSD_D_INNER // LANES
    for g in range(SSD_GROUPS):
        b_ref[:, g * SSD_STATE:(g + 1) * SSD_STATE] = xbc_ref[x_slabs + g].astype(BF16)
        c_ref[:, g * SSD_STATE:(g + 1) * SSD_STATE] = xbc_ref[x_slabs + SSD_GROUPS + g].astype(BF16)

    rid = lax.broadcasted_iota(jnp.int32, (tile, tile), 0)
    cid = lax.broadcasted_iota(jnp.int32, (tile, tile), 1)

    def gla_stages():
        n_chunks = tile // CHUNK
        chunk_of = lambda idx: lax.shift_right_logical(idx, CHUNK.bit_length() - 1)
        tri_blocks = ((rid >= cid) & (chunk_of(rid) == chunk_of(cid))).astype(BF16)
        bcum = _cumsum_rows(tri_blocks, la_ref[...])
        blast = jnp.concatenate(
            [jnp.broadcast_to(bcum[(i + 1) * CHUNK - 1:(i + 1) * CHUNK, :], (CHUNK, GLA_DK))
             for i in range(n_chunks)], axis=0)
        kk = k_ref[...]
        yield
        qt_ref[...] = (q_ref[...] * (GLA_HEAD_K ** -0.5) * jnp.exp(bcum)).astype(BF16)
        yield
        kt_ref[...] = (kk * jnp.exp(-bcum)).astype(BF16)
        yield
        kh_ref[...] = (kk * jnp.exp(blast - bcum)).astype(BF16)
        dec_t = jnp.exp(blast).T
        causal = (lax.broadcasted_iota(jnp.int32, (CHUNK, CHUNK), 0)
                  >= lax.broadcasted_iota(jnp.int32, (CHUNK, CHUNK), 1))
        ksls = [slice(j * GLA_HEAD_K, (j + 1) * GLA_HEAD_K) for j in range(GLA_HEADS)]
        vsls = [slice(j * GLA_HEAD_V, (j + 1) * GLA_HEAD_V) for j in range(GLA_HEADS)]
        for i in range(n_chunks):
            rs = slice(i * CHUNK, (i + 1) * CHUNK)
            for j in range(GLA_HEADS):
                scores = _dot_nt(qt_ref[rs, ksls[j]], kt_ref[rs, ksls[j]])
                att_ref[i * GLA_HEADS + j] = jnp.where(causal, scores, 0.0).astype(BF16)
            yield
        states = [gstate_ref[ksl, :] for ksl in ksls]
        for i in range(n_chunks):
            rs = slice(i * CHUNK, (i + 1) * CHUNK)
            for j in range(GLA_HEADS):
                sb_ref[i, ksls[j], :] = states[j].astype(BF16)
                update = _dot_tn(kh_ref[rs, ksls[j]], v_ref[rs, vsls[j]])
                states[j] = states[j] * dec_t[ksls[j], i * CHUNK:i * CHUNK + 1] + update
            yield
        for j in range(GLA_HEADS):
            gstate_ref[ksls[j], :] = states[j]
        for i in range(n_chunks):
            rs = slice(i * CHUNK, (i + 1) * CHUNK)
            for j in range(GLA_HEADS):
                o_ref[rs, vsls[j]] = (_dot(att_ref[i * GLA_HEADS + j], v_ref[rs, vsls[j]])
                                      + _dot(qt_ref[rs, ksls[j]], sb_ref[i, ksls[j], :]))
            yield
        for j in range(GLA_HEADS):
            yg_ref[:, vsls[j]] = (_rms(o_ref[:, vsls[j]], glanorm_ref[...])
                                  * _silu(r_ref[:, vsls[j]])).astype(BF16)
            yield

    gla = gla_stages()

    upper = rid <= cid
    a_t = a_ref[...].T
    acs_t = _cumsum_lanes(a_t, upper.astype(BF16))
    acs = acs_t.T
    dt_t = dt_ref[...].T
    xs_t = [xbc_ref[sl].T for sl in range(x_slabs)]
    heads_per_slab = LANES // SSD_HEAD_DIM
    hp = SSD_HEADS_PER_GROUP * SSD_HEAD_DIM
    for g in range(SSD_GROUPS):
        nsl = slice(g * SSD_STATE, (g + 1) * SSD_STATE)
        bg = b_ref[:, nsl]
        cg = c_ref[:, nsl]
        cb_t = jnp.where(upper, _dot_nt(bg, cg), 0.0)
        state = sstate_ref[g * hp:(g + 1) * hp, :]
        y_off_t = _dot_nt(state.astype(BF16), cg)
        for r in range(SSD_HEADS_PER_GROUP):
            h = g * SSD_HEADS_PER_GROUP + r
            psl = slice(h * SSD_HEAD_DIM, (h + 1) * SSD_HEAD_DIM)
            rsl = slice(r * SSD_HEAD_DIM, (r + 1) * SSD_HEAD_DIM)
            col = acs[:, h:h + 1]
            row = acs_t[h:h + 1, :]
            last = row[:, tile - 1:tile]
            m_t = (jnp.exp(jnp.minimum(row - col, 0.0)) * cb_t).astype(BF16)
            in_slab = (h % heads_per_slab) * SSD_HEAD_DIM
            xh_t = xs_t[h // heads_per_slab][in_slab:in_slab + SSD_HEAD_DIM, :]
            xdt_t = xh_t * dt_t[h:h + 1, :]
            y_diag_t = _dot(xdt_t.astype(BF16), m_t)
            yt_ref[psl, :] = (y_diag_t + y_off_t[rsl, :] * jnp.exp(row)
                              + dskip_ref[h:h + 1, :] * xh_t)
            xd_ref[rsl, :] = (xdt_t * jnp.exp(last - row)).astype(BF16)
            sstate_ref[psl, :] = state[rsl, :] * jnp.exp(last)
            emit_proj()
            next(gla, None)
        sstate_ref[g * hp:(g + 1) * hp, :] += _dot(xd_ref[...], bg)
    emit_proj(len(deferred))
    yz = yt_ref[...].T * _silu(z_ref[...])
    gn = SSD_D_INNER // SSD_GROUPS
    for g in range(SSD_GROUPS):
        csl = slice(g * gn, (g + 1) * gn)
        ys_ref[:, csl] = _rms(yz[:, csl], ssdnorm_ref[:, csl]).astype(BF16)
    merged_ssd = jax.nn.sigmoid(gs_ref[...]) * _dot(ys_ref[...], wus_ref[:, :D_MODEL])

    for _ in gla:
        pass

    merged = merged_ssd + jax.nn.sigmoid(gg_ref[...]) * _dot(yg_ref[...], wug_ref[:, :D_MODEL])
    h_ref[...] = x_ref[...] + _dot(merged.astype(BF16), wo_ref[:, :D_MODEL])


def _memkv_kernel(mem_ref, g_ref, wkv_ref, wq_ref, wo_ref, wqk_ref, vwo_ref):
    mem_len = mem_ref.shape[0]
    m = _rms(mem_ref[...], g_ref[...]).astype(BF16)
    kv = _dot(m, wkv_ref[...])
    for j in range(XATTN_HEADS):
        dsl = slice(j * XATTN_HEAD_DIM, (j + 1) * XATTN_HEAD_DIM)
        msl = slice(j * mem_len, (j + 1) * mem_len)
        k_j = kv[:, dsl].astype(BF16)
        v_j = kv[:, D_MODEL + j * XATTN_HEAD_DIM:D_MODEL + (j + 1) * XATTN_HEAD_DIM].astype(BF16)
        wqk_ref[:, msl] = (_dot_nt(wq_ref[:, dsl], k_j) * (XATTN_HEAD_DIM ** -0.5)).astype(BF16)
        vwo_ref[msl, 0:D_MODEL] = _dot(v_j, wo_ref[dsl, :]).astype(BF16)
    att_width = XATTN_HEADS * mem_len
    for ref, used in ((wqk_ref, att_width), (vwo_ref, D_MODEL)):
        if ref.shape[1] > used:
            ref[:, used:] = jnp.zeros((ref.shape[0], ref.shape[1] - used), BF16)


def _tail_kernel(h_ref, gx_ref, wqk_ref, vwo_ref, gf_ref, wfi_ref, wfo_ref, gfin_ref,
                 out_ref, p_ref, act_ref):
    att_width = p_ref.shape[1]
    mem_len = att_width // XATTN_HEADS
    ff_block = D_FF // 2
    rows_per_group = h_ref.shape[0] // TAIL_ROW_GROUPS
    groups = [slice(g * rows_per_group, (g + 1) * rows_per_group) for g in range(TAIL_ROW_GROUPS)]
    h1 = [h_ref[rows, :] for rows in groups]
    sc = [_dot(_rms(h, gx_ref[...]).astype(BF16), wqk_ref[:, :att_width]) for h in h1]
    for rows, s in zip(groups, sc):
        for j in range(XATTN_HEADS):
            msl = slice(j * mem_len, (j + 1) * mem_len)
            e = jnp.exp(s[:, msl] - jnp.max(s[:, msl], axis=-1, keepdims=True))
            p_ref[rows, msl] = (e / jnp.sum(e, axis=-1, keepdims=True)).astype(BF16)
    h2 = [h + _dot(p_ref[rows, :], vwo_ref[:, :D_MODEL]) for rows, h in zip(groups, h1)]
    n3 = [_rms(h, gf_ref[...]).astype(BF16) for h in h2]
    for cb in range(D_FF // ff_block):
        for rows, n in zip(groups, n3):
            gate = _dot(n, wfi_ref[:, cb * ff_block:(cb + 1) * ff_block])
            up = _dot(n, wfi_ref[:, D_FF + cb * ff_block:D_FF + (cb + 1) * ff_block])
            act_ref[rows, cb * ff_block:(cb + 1) * ff_block] = (_silu(gate) * up).astype(BF16)
    for rows, h in zip(groups, h2):
        h3 = h + _dot(act_ref[rows, :], wfo_ref[:, :D_MODEL])
        out_ref[rows, :] = _rms(h3, gfin_ref[...])


IN_BOUNDS = tuple(sum(IN_SIZES[:i]) for i in range(len(IN_SIZES) + 1))
IN_RUNS = ((OFF_Z, IN_BOUNDS[0], IN_BOUNDS[2] - IN_BOUNDS[0]),
           (OFF_Q, IN_BOUNDS[3], IN_BOUNDS[7] - IN_BOUNDS[3]),
           (OFF_GS, IN_BOUNDS[8], IN_BOUNDS[10] - IN_BOUNDS[8]))
IN_DT_COL = IN_BOUNDS[2]
IN_A1_COL = IN_BOUNDS[7]
assert all(dst % PREP_SLAB == 0 and width % PREP_SLAB == 0 for dst, _, width in IN_RUNS)
assert OFF_SMALL == (PREP_STEPS - 1) * PREP_SLAB and IN_WIDTH_PADDED == PREP_STEPS * PREP_SLAB
assert _pitch(IN_WIDTH_ALLOC) == IN_WIDTH_ALLOC


def _prep_src_row(i):
    unit = SSD_HEADS
    start = i * (PREP_SLAB // unit)
    prev_shift = 0
    for dst, src, _ in IN_RUNS:
        start = start + jnp.where(i >= dst // PREP_SLAB, (src - dst - prev_shift) // unit, 0)
        prev_shift = src - dst
    last = (IN_BOUNDS[-1] - PREP_SLAB) // unit
    return jnp.minimum(start, last) * unit


def _prep_kernel(wint_ref, dt_ref, a1_ref, *refs):
    n_plain = (len(refs) - 1) // 2
    plain_in, owin_ref, plain_out = refs[:n_plain], refs[n_plain], refs[n_plain + 1:]
    i = pl.program_id(0)

    @pl.when(i < PREP_STEPS - 1)
    def _():
        owin_ref[...] = wint_ref[...].T.astype(BF16)

    @pl.when(i == PREP_STEPS - 1)
    def _():
        pad = jnp.zeros((LANES - SSD_HEADS - GLA_GATE_RANK, D_MODEL), F32)
        small_t = jnp.concatenate([dt_ref[...], a1_ref[...], pad], axis=0)
        owin_ref[:, 0:LANES] = small_t.T.astype(BF16)
        owin_ref[:, LANES:] = jnp.zeros((D_MODEL, PREP_SLAB - LANES), BF16)

    @pl.when(i == PREP_STEPS)
    def _():
        owin_ref[...] = jnp.zeros(owin_ref.shape, BF16)

    for src_ref, dst_ref in zip(plain_in, plain_out):
        cols = src_ref.shape[1]
        dst_ref[:, 0:cols] = src_ref[...].astype(BF16)
        if dst_ref.shape[1] > cols:
            dst_ref[:, cols:] = jnp.zeros((dst_ref.shape[0], dst_ref.shape[1] - cols), BF16)


def _resident(shape):
    return pl.BlockSpec(shape, lambda b, s: (0,) * len(shape), pipeline_mode=pl.Buffered(1))


def _row(v, width=None):
    v = v.reshape(1, -1).astype(F32)
    if width is not None and v.shape[1] < width:
        v = jnp.pad(v, ((0, 0), (0, width - v.shape[1])))
    return v


@functools.partial(jax.jit, static_argnames=("tile", "tail_tile"))
def _forward(x, mem, norm_mix, w_in, ssd_conv_w, ssd_conv_b, ssd_dt_bias, ssd_A_log, ssd_D, ssd_norm,
             gla_w_a2, gla_b_a, gla_norm, w_up_ssd, w_up_gla, w_o, norm_xattn, norm_mem, w_xq, w_xkv,
             w_xo, norm_ffn, w_ffn_in, w_ffn_out, norm_final, *, tile, tail_tile):
    batch, seq, _ = x.shape
    mem_len = mem.shape[1]
    assert seq % tile == 0 and tile % CHUNK == 0
    grid = (batch, seq // tile)

    w_in_t = jnp.swapaxes(w_in[0], 0, 1)
    plain = [w_up_ssd, w_up_gla, w_o, w_xq, w_xkv, w_xo, w_ffn_in, w_ffn_out]
    row_of = lambda i: jnp.minimum(i, PREP_STEPS - 1)
    row_block_in = lambda a: pl.BlockSpec((None, a.shape[1] // PREP_STEPS, a.shape[2]),
                                          lambda i: (0, row_of(i), 0))
    row_block_out = lambda a: pl.BlockSpec((a.shape[0] // PREP_STEPS, a.shape[1]),
                                           lambda i: (row_of(i), 0))
    rows_at = lambda n, start: pl.BlockSpec((pl.Element(n), pl.Element(D_MODEL)),
                                            lambda i: (start(i), 0))
    streamed = (w_up_ssd, w_up_gla, w_o, w_ffn_in, w_ffn_out)
    outs = [jax.ShapeDtypeStruct((a.shape[1], _pitch(a.shape[2]) if any(a is s for s in streamed)
                                  else a.shape[2]), BF16) for a in plain]
    (w_in_r, w_up_ssd_b, w_up_gla_b, w_o_b, w_xq_b, w_xkv_b, w_xo_b, w_ffn_in_b,
     w_ffn_out_b) = pl.pallas_call(
        _prep_kernel,
        grid=(PREP_STEPS + 1,),
        in_specs=([rows_at(PREP_SLAB, _prep_src_row),
                   rows_at(SSD_HEADS, lambda i: IN_DT_COL), rows_at(GLA_GATE_RANK, lambda i: IN_A1_COL)]
                  + [row_block_in(a) for a in plain]),
        out_specs=([pl.BlockSpec((D_MODEL, PREP_SLAB), lambda i: (0, i))]
                   + [row_block_out(a) for a in outs]),
        out_shape=[jax.ShapeDtypeStruct((D_MODEL, IN_WIDTH_ALLOC), BF16)] + outs,
        compiler_params=pltpu.CompilerParams(
            dimension_semantics=("arbitrary",), vmem_limit_bytes=VMEM_LIMIT_BYTES),
        name="prep",
    )(w_in_t, w_in_t, w_in_t, *plain)
    wa2 = jnp.zeros((LANES, GLA_DK), F32).at[SMALL_A1:SMALL_A1 + GLA_GATE_RANK].set(gla_w_a2[0]).astype(BF16)
    dskip = jnp.broadcast_to(ssd_D[0].astype(F32)[:, None], (SSD_HEADS, tile))

    tok_spec = pl.BlockSpec((None, tile, D_MODEL), lambda b, s: (b, s, 0))

    mixer_inputs = [
        (x, tok_spec),
        (_row(norm_mix[0]), _resident((1, D_MODEL))),
        (w_in_r, _resident((D_MODEL, IN_WIDTH_ALLOC))),
        (ssd_conv_w[0].reshape(SSD_CONV, SSD_CONV_CH), _resident((SSD_CONV, SSD_CONV_CH))),
        (_row(ssd_conv_b[0]), _resident((1, SSD_CONV_CH))),
        (_row(ssd_dt_bias[0], LANES), _resident((1, LANES))),
        (_row(ssd_A_log[0], LANES), _resident((1, LANES))),
        (dskip, _resident((SSD_HEADS, tile))),
        (_row(ssd_norm[0]), _resident((1, SSD_D_INNER))),
        (wa2, _resident((LANES, GLA_DK))),
        (_row(gla_b_a[0]), _resident((1, GLA_DK))),
        (_row(gla_norm[0]), _resident((1, GLA_HEAD_V))),
        (w_up_ssd_b, _resident(w_up_ssd_b.shape)),
        (w_up_gla_b, _resident(w_up_gla_b.shape)),
        (w_o_b, _resident(w_o_b.shape)),
    ]
    mixer_scratch = [
        pltpu.VMEM((tile, SSD_D_INNER), F32),
        pltpu.VMEM((CONV_SLABS, tile + CONV_HALO, LANES), F32),
        pltpu.VMEM((tile, GLA_DK), F32),
        pltpu.VMEM((tile, GLA_DK), F32),
        pltpu.VMEM((tile, GLA_DV), BF16),
        pltpu.VMEM((tile, GLA_DV), F32),
        pltpu.VMEM((tile, D_MODEL), F32),
        pltpu.VMEM((tile, D_MODEL), F32),
        pltpu.VMEM((CONV_SLABS, tile, LANES), F32),
        pltpu.VMEM((tile, SSD_GROUPS * SSD_STATE), BF16),
        pltpu.VMEM((tile, SSD_GROUPS * SSD_STATE), BF16),
        pltpu.VMEM((tile, LANES), F32),
        pltpu.VMEM((tile, LANES), F32),
        pltpu.VMEM((tile, GLA_DK), F32),
        pltpu.VMEM((SSD_D_INNER, tile), F32),
        pltpu.VMEM((SSD_HEADS_PER_GROUP * SSD_HEAD_DIM, tile), BF16),
        pltpu.VMEM((tile, SSD_D_INNER), BF16),
        pltpu.VMEM((tile, GLA_DK), BF16),
        pltpu.VMEM((tile, GLA_DK), BF16),
        pltpu.VMEM((tile, GLA_DK), BF16),
        pltpu.VMEM((tile // CHUNK * GLA_HEADS, CHUNK, CHUNK), BF16),
        pltpu.VMEM((tile // CHUNK, GLA_DK, GLA_HEAD_V), BF16),
        pltpu.VMEM((tile, GLA_DV), F32),
        pltpu.VMEM((tile, GLA_DV), BF16),
        pltpu.VMEM((SSD_D_INNER, SSD_STATE), F32),
        pltpu.VMEM((GLA_DK, GLA_HEAD_V), F32),
    ]
    h1 = pl.pallas_call(
        functools.partial(_mixer_kernel, tile=tile),
        grid=grid,
        in_specs=[spec for _, spec in mixer_inputs],
        out_specs=tok_spec,
        out_shape=jax.ShapeDtypeStruct(x.shape, F32),
        scratch_shapes=mixer_scratch,
        compiler_params=pltpu.CompilerParams(
            dimension_semantics=("arbitrary", "arbitrary"), vmem_limit_bytes=VMEM_LIMIT_BYTES),
        name="mixer",
    )(*[a for a, _ in mixer_inputs])

    att_width = XATTN_HEADS * mem_len
    wqk_shape = (D_MODEL, _pitch(att_width))
    vwo_shape = (att_width, _pitch(D_MODEL))
    mem_spec = pl.BlockSpec((None, mem_len, D_MODEL), lambda b: (b, 0, 0))
    whole = lambda shape: pl.BlockSpec(shape, lambda b: (0,) * len(shape))
    wqk, vwo = pl.pallas_call(
        _memkv_kernel,
        grid=(batch,),
        in_specs=[mem_spec, whole((1, D_MODEL)), whole((D_MODEL, 2 * D_MODEL)),
                  whole((D_MODEL, D_MODEL)), whole((D_MODEL, D_MODEL))],
        out_specs=[pl.BlockSpec((None,) + wqk_shape, lambda b: (b, 0, 0)),
                   pl.BlockSpec((None,) + vwo_shape, lambda b: (b, 0, 0))],
        out_shape=[jax.ShapeDtypeStruct((batch,) + wqk_shape, BF16),
                   jax.ShapeDtypeStruct((batch,) + vwo_shape, BF16)],
        compiler_params=pltpu.CompilerParams(
            dimension_semantics=("arbitrary",), vmem_limit_bytes=VMEM_LIMIT_BYTES),
        name="memkv",
    )(mem, _row(norm_mem[0]), w_xkv_b, w_xq_b, w_xo_b)

    assert seq % tail_tile == 0
    tail_spec = pl.BlockSpec((None, tail_tile, D_MODEL), lambda b, s: (b, s, 0))
    out = pl.pallas_call(
        _tail_kernel,
        grid=(batch, seq // tail_tile),
        in_specs=[tail_spec,
                  _resident((1, D_MODEL)),
                  pl.BlockSpec((None,) + wqk_shape, lambda b, s: (b, 0, 0)),
                  pl.BlockSpec((None,) + vwo_shape, lambda b, s: (b, 0, 0)),
                  _resident((1, D_MODEL)),
                  _resident(w_ffn_in_b.shape),
                  _resident(w_ffn_out_b.shape),
                  _resident((1, D_MODEL))],
        out_specs=tail_spec,
        out_shape=jax.ShapeDtypeStruct(x.shape, F32),
        scratch_shapes=[pltpu.VMEM((tail_tile, att_width), BF16),
                        pltpu.VMEM((tail_tile, D_FF), BF16)],
        compiler_params=pltpu.CompilerParams(
            dimension_semantics=("arbitrary", "arbitrary"), vmem_limit_bytes=VMEM_LIMIT_BYTES),
        name="tail",
    )(h1, _row(norm_xattn[0]), wqk, vwo, _row(norm_ffn[0]), w_ffn_in_b, w_ffn_out_b, _row(norm_final))
    return out


def kernel(x, mem, norm_mix, w_in, ssd_conv_w, ssd_conv_b, ssd_dt_bias, ssd_A_log, ssd_D, ssd_norm,
           gla_w_a2, gla_b_a, gla_norm, w_up_ssd, w_up_gla, w_o, norm_xattn, norm_mem, w_xq, w_xkv,
           w_xo, norm_ffn, w_ffn_in, w_ffn_out, norm_final):
    return _forward(x, mem, norm_mix, w_in, ssd_conv_w, ssd_conv_b, ssd_dt_bias, ssd_A_log, ssd_D,
                    ssd_norm, gla_w_a2, gla_b_a, gla_norm, w_up_ssd, w_up_gla, w_o, norm_xattn,
                    norm_mem, w_xq, w_xkv, w_xo, norm_ffn, w_ffn_in, w_ffn_out, norm_final,
                    tile=MIXER_TILE, tail_tile=TAIL_TILE)
```

```python
import functools

import jax
import jax.numpy as jnp
from jax import lax
from jax.experimental import pallas as pl
from jax.experimental.pallas import tpu as pltpu

F32 = jnp.float32
BF16 = jnp.bfloat16

D_MODEL = 1024
EPS = 1e-6
LOG2_E = 1.4426950408889634
CHUNK = 64
SSD_D_INNER = 1024
SSD_HEAD_DIM = 64
SSD_HEADS = 16
SSD_GROUPS = 2
SSD_HEADS_PER_GROUP = SSD_HEADS // SSD_GROUPS
SSD_STATE = 128
SSD_CONV = 4
SSD_CONV_CH = SSD_D_INNER + 2 * SSD_GROUPS * SSD_STATE
GLA_HEADS = 4
GLA_DK = 512
GLA_DV = 1024
GLA_HEAD_K = GLA_DK // GLA_HEADS
GLA_HEAD_V = GLA_DV // GLA_HEADS
GLA_GATE_RANK = 16
GLA_TAU = 16.0
XATTN_HEADS = 4
XATTN_HEAD_DIM = D_MODEL // XATTN_HEADS
D_FF = 2816
IN_SIZES = (SSD_D_INNER, SSD_CONV_CH, SSD_HEADS, GLA_DK, GLA_DK, GLA_DV, GLA_DV, GLA_GATE_RANK,
            D_MODEL, D_MODEL)

LANES = 128
SUBLANES = 8
VMEM_LIMIT_BYTES = 56 * 1024 * 1024
MIXER_TILE = 256
TAIL_TILE = 1024
TAIL_ROW_GROUPS = 4
PREP_STEPS = 16
PREP_SLAB = 512

OFF_Z = 0
OFF_XBC = OFF_Z + SSD_D_INNER
OFF_Q = OFF_XBC + SSD_CONV_CH
OFF_K = OFF_Q + GLA_DK
OFF_V = OFF_K + GLA_DK
OFF_R = OFF_V + GLA_DV
OFF_GS = OFF_R + GLA_DV
OFF_GG = OFF_GS + D_MODEL
OFF_SMALL = OFF_GG + D_MODEL
IN_WIDTH_PADDED = OFF_SMALL + PREP_SLAB
IN_WIDTH_ALLOC = IN_WIDTH_PADDED + PREP_SLAB
SMALL_DT = 0
SMALL_A1 = SSD_HEADS

CONV_HALO = SUBLANES
CONV_SLABS = SSD_CONV_CH // LANES
CONV_PHASES = 4
assert SSD_STATE == LANES and LANES % SSD_HEAD_DIM == 0


def _pitch(width):
    return width + LANES if (width // LANES) % SUBLANES == 0 else width


def _dot(a, b):
    return jnp.dot(a, b, preferred_element_type=F32)


def _dot_nt(a, b):
    return lax.dot_general(a, b, (((1,), (1,)), ((), ())), preferred_element_type=F32)


def _dot_tn(a, b):
    return lax.dot_general(a, b, (((0,), (0,)), ((), ())), preferred_element_type=F32)


def _rms(x, g):
    return x * lax.rsqrt(jnp.mean(x * x, axis=-1, keepdims=True) + EPS) * g


def _silu(x):
    return x * jax.nn.sigmoid(x)


def _split3(x):
    hi = x.astype(BF16)
    r1 = x - hi.astype(F32)
    mid = r1.astype(BF16)
    lo = (r1 - mid.astype(F32)).astype(BF16)
    return hi, mid, lo


def _cumsum_rows(tri, x):
    hi, mid, lo = _split3(x)
    return _dot(tri, hi) + _dot(tri, mid) + _dot(tri, lo)


def _cumsum_lanes(x, upper):
    hi, mid, lo = _split3(x)
    return _dot(hi, upper) + _dot(mid, upper) + _dot(lo, upper)


def _mixer_kernel(x_ref, gmix_ref, win_ref, convw_ref, convb_ref, dtb_ref, alog_ref, dskip_ref,
                  ssdnorm_ref, wa2_ref, ba_ref, glanorm_ref, wus_ref, wug_ref, wo_ref,
                  h_ref,
                  z_ref, xpad_ref, q_ref, k_ref, v_ref, r_ref, gs_ref, gg_ref,
                  xbc_ref, b_ref, c_ref, dt_ref, a_ref, la_ref, yt_ref, xd_ref, ys_ref,
                  qt_ref, kt_ref, kh_ref, att_ref, sb_ref, o_ref, yg_ref,
                  sstate_ref, gstate_ref, *, tile):
    s = pl.program_id(1)

    @pl.when(s == 0)
    def _():
        sstate_ref[...] = jnp.zeros_like(sstate_ref)
        gstate_ref[...] = jnp.zeros_like(gstate_ref)
        xpad_ref[:, 0:CONV_HALO, :] = jnp.zeros((CONV_SLABS, CONV_HALO, LANES), F32)

    n = _rms(x_ref[...], gmix_ref[...]).astype(BF16)

    def proj(off, width):
        return _dot(n, win_ref[:, off:off + width])

    slab = 512
    deferred = [(dst, off, c0)
                for dst, off, width in ((q_ref, OFF_Q, GLA_DK), (k_ref, OFF_K, GLA_DK),
                                        (v_ref, OFF_V, GLA_DV), (r_ref, OFF_R, GLA_DV),
                                        (z_ref, OFF_Z, SSD_D_INNER), (gs_ref, OFF_GS, D_MODEL),
                                        (gg_ref, OFF_GG, D_MODEL))
                for c0 in range(0, width, slab)]

    def emit_proj(count=1):
        for _ in range(count):
            if deferred:
                dst, off, c0 = deferred.pop(0)
                dst[:, c0:c0 + slab] = proj(off + c0, slab).astype(dst.dtype)

    for c0 in range(0, SSD_CONV_CH, slab):
        xbc = proj(OFF_XBC + c0, slab)
        for t in range(slab // LANES):
            xpad_ref[c0 // LANES + t, CONV_HALO:CONV_HALO + tile, :] = xbc[:, t * LANES:(t + 1) * LANES]
    small = proj(OFF_SMALL, LANES)

    dt = jax.nn.softplus(small + dtb_ref[...])
    dt_ref[...] = dt
    a_ref[...] = dt * (-jnp.exp(alog_ref[...]))
    logits = _dot(small.astype(BF16), wa2_ref[...]) + ba_ref[...]
    la_ref[...] = jax.nn.log_sigmoid(logits) / GLA_TAU

    rows_per_phase = tile // CONV_PHASES
    first = CONV_HALO - (SSD_CONV - 1)
    for sl in range(CONV_SLABS):
        cols = slice(sl * LANES, (sl + 1) * LANES)
        taps = {d: xpad_ref[sl, pl.ds(first + d, rows_per_phase, stride=CONV_PHASES), :]
                for d in range(CONV_PHASES + SSD_CONV - 1)}
        for phase in range(CONV_PHASES):
            acc = convb_ref[:, cols]
            for j in range(SSD_CONV):
                acc = acc + convw_ref[j:j + 1, cols] * taps[phase + j]
            xbc_ref[sl, pl.ds(phase, rows_per_phase, stride=CONV_PHASES), :] = _silu(acc)
        if sl % 2 == 1:
            emit_proj()
    xpad_ref[:, 0:CONV_HALO, :] = xpad_ref[:, tile:tile + CONV_HALO, :]
    x_slabs = SSD_D_INNER // LANES
    for g in range(SSD_GROUPS):
        b_ref[:, g * SSD_STATE:(g + 1) * SSD_STATE] = xbc_ref[x_slabs + g].astype(BF16)
        c_ref[:, g * SSD_STATE:(g + 1) * SSD_STATE] = xbc_ref[x_slabs + SSD_GROUPS + g].astype(BF16)

    rid = lax.broadcasted_iota(jnp.int32, (tile, tile), 0)
    cid = lax.broadcasted_iota(jnp.int32, (tile, tile), 1)

    def gla_stages():
        n_chunks = tile // CHUNK
        chunk_of = lambda idx: lax.shift_right_logical(idx, CHUNK.bit_length() - 1)
        tri_blocks = ((rid >= cid) & (chunk_of(rid) == chunk_of(cid))).astype(BF16)
        bcum = _cumsum_rows(tri_blocks, la_ref[...]) * LOG2_E
        blast = jnp.concatenate(
            [jnp.broadcast_to(bcum[(i + 1) * CHUNK - 1:(i + 1) * CHUNK, :], (CHUNK, GLA_DK))
             for i in range(n_chunks)], axis=0)
        kk = k_ref[...]
        yield
        qt_ref[...] = (q_ref[...] * (GLA_HEAD_K ** -0.5) * jnp.exp2(bcum)).astype(BF16)
        yield
        kt_ref[...] = (kk * jnp.exp2(-bcum)).astype(BF16)
        yield
        kh_ref[...] = (kk * jnp.exp2(blast - bcum)).astype(BF16)
        dec_t = jnp.exp2(blast).T
        causal = (lax.broadcasted_iota(jnp.int32, (CHUNK, CHUNK), 0)
                  >= lax.broadcasted_iota(jnp.int32, (CHUNK, CHUNK), 1))
        ksls = [slice(j * GLA_HEAD_K, (j + 1) * GLA_HEAD_K) for j in range(GLA_HEADS)]
        vsls = [slice(j * GLA_HEAD_V, (j + 1) * GLA_HEAD_V) for j in range(GLA_HEADS)]
        for i in range(n_chunks):
            rs = slice(i * CHUNK, (i + 1) * CHUNK)
            for j in range(GLA_HEADS):
                scores = _dot_nt(qt_ref[rs, ksls[j]], kt_ref[rs, ksls[j]])
                att_ref[i * GLA_HEADS + j] = jnp.where(causal, scores, 0.0).astype(BF16)
            yield
        states = [gstate_ref[ksl, :] for ksl in ksls]
        for i in range(n_chunks):
            rs = slice(i * CHUNK, (i + 1) * CHUNK)
            for j in range(GLA_HEADS):
                sb_ref[i, ksls[j], :] = states[j].astype(BF16)
                update = _dot_tn(kh_ref[rs, ksls[j]], v_ref[rs, vsls[j]])
                states[j] = states[j] * dec_t[ksls[j], i * CHUNK:i * CHUNK + 1] + update
            yield
        for j in range(GLA_HEADS):
            gstate_ref[ksls[j], :] = states[j]
        for i in range(n_chunks):
            rs = slice(i * CHUNK, (i + 1) * CHUNK)
            for j in range(GLA_HEADS):
                o_ref[rs, vsls[j]] = (_dot(att_ref[i * GLA_HEADS + j], v_ref[rs, vsls[j]])
                                      + _dot(qt_ref[rs, ksls[j]], sb_ref[i, ksls[j], :]))
            yield
        for j in range(GLA_HEADS):
            yg_ref[:, vsls[j]] = (_rms(o_ref[:, vsls[j]], glanorm_ref[...])
                                  * _silu(r_ref[:, vsls[j]])).astype(BF16)
            yield

    gla = gla_stages()

    upper = rid <= cid
    a_t = a_ref[...].T
    acs_t = _cumsum_lanes(a_t, upper.astype(BF16)) * LOG2_E
    acs = acs_t.T
    dt_t = dt_ref[...].T
    xs_t = [xbc_ref[sl].T for sl in range(x_slabs)]
    heads_per_slab = LANES // SSD_HEAD_DIM
    hp = SSD_HEADS_PER_GROUP * SSD_HEAD_DIM
    for g in range(SSD_GROUPS):
        nsl = slice(g * SSD_STATE, (g + 1) * SSD_STATE)
        bg = b_ref[:, nsl]
        cg = c_ref[:, nsl]
        cb_t = jnp.where(upper, _dot_nt(bg, cg), 0.0)
        state = sstate_ref[g * hp:(g + 1) * hp, :]
        y_off_t = _dot_nt(state.astype(BF16), cg)
        for r in range(SSD_HEADS_PER_GROUP):
            h = g * SSD_HEADS_PER_GROUP + r
            psl = slice(h * SSD_HEAD_DIM, (h + 1) * SSD_HEAD_DIM)
            rsl = slice(r * SSD_HEAD_DIM, (r + 1) * SSD_HEAD_DIM)
            col = acs[:, h:h + 1]
            row = acs_t[h:h + 1, :]
            last = row[:, tile - 1:tile]
            m_t = (jnp.exp2(jnp.minimum(row - col, 0.0)) * cb_t).astype(BF16)
            in_slab = (h % heads_per_slab) * SSD_HEAD_DIM
            xh_t = xs_t[h // heads_per_slab][in_slab:in_slab + SSD_HEAD_DIM, :]
            xdt_t = xh_t * dt_t[h:h + 1, :]
            y_diag_t = _dot(xdt_t.astype(BF16), m_t)
            yt_ref[psl, :] = (y_diag_t + y_off_t[rsl, :] * jnp.exp2(row)
                              + dskip_ref[h:h + 1, :] * xh_t)
            xd_ref[rsl, :] = (xdt_t * jnp.exp2(last - row)).astype(BF16)
            sstate_ref[psl, :] = state[rsl, :] * jnp.exp2(last)
            emit_proj()
            next(gla, None)
        sstate_ref[g * hp:(g + 1) * hp, :] += _dot(xd_ref[...], bg)
    emit_proj(len(deferred))
    yz = yt_ref[...].T * _silu(z_ref[...])
    gn = SSD_D_INNER // SSD_GROUPS
    for g in range(SSD_GROUPS):
        csl = slice(g * gn, (g + 1) * gn)
        ys_ref[:, csl] = _rms(yz[:, csl], ssdnorm_ref[:, csl]).astype(BF16)
    merged_ssd = jax.nn.sigmoid(gs_ref[...]) * _dot(ys_ref[...], wus_ref[:, :D_MODEL])

    for _ in gla:
        pass

    merged = merged_ssd + jax.nn.sigmoid(gg_ref[...]) * _dot(yg_ref[...], wug_ref[:, :D_MODEL])
    h_ref[...] = x_ref[...] + _dot(merged.astype(BF16), wo_ref[:, :D_MODEL])


def _memkv_kernel(mem_ref, g_ref, wkv_ref, wq_ref, wo_ref, wqk_ref, vwo_ref):
    mem_len = mem_ref.shape[0]
    m = _rms(mem_ref[...], g_ref[...]).astype(BF16)
    kv = _dot(m, wkv_ref[...].astype(BF16))
    for j in range(XATTN_HEADS):
        dsl = slice(j * XATTN_HEAD_DIM, (j + 1) * XATTN_HEAD_DIM)
        msl = slice(j * mem_len, (j + 1) * mem_len)
        k_j = kv[:, dsl].astype(BF16)
        v_j = kv[:, D_MODEL + j * XATTN_HEAD_DIM:D_MODEL + (j + 1) * XATTN_HEAD_DIM].astype(BF16)
        wq_j = wq_ref[:, dsl].astype(BF16)
        wqk_ref[:, msl] = (_dot_nt(wq_j, k_j) * (XATTN_HEAD_DIM ** -0.5)).astype(BF16)
        vwo_ref[msl, 0:D_MODEL] = _dot(v_j, wo_ref[dsl, :].astype(BF16)).astype(BF16)
    att_width = XATTN_HEADS * mem_len
    for ref, used in ((wqk_ref, att_width), (vwo_ref, D_MODEL)):
        if ref.shape[1] > used:
            ref[:, used:] = jnp.zeros((ref.shape[0], ref.shape[1] - used), BF16)


def _tail_kernel(h_ref, gx_ref, wqk_ref, vwo_ref, gf_ref, wfi_ref, wfo_ref, gfin_ref,
                 out_ref, p_ref, act_ref):
    att_width = p_ref.shape[1]
    mem_len = att_width // XATTN_HEADS
    ff_block = D_FF // 2
    rows_per_group = h_ref.shape[0] // TAIL_ROW_GROUPS
    groups = [slice(g * rows_per_group, (g + 1) * rows_per_group) for g in range(TAIL_ROW_GROUPS)]
    h1 = [h_ref[rows, :] for rows in groups]
    sc = [_dot(_rms(h, gx_ref[...]).astype(BF16), wqk_ref[:, :att_width]) for h in h1]
    for rows, s in zip(groups, sc):
        for j in range(XATTN_HEADS):
            msl = slice(j * mem_len, (j + 1) * mem_len)
            e = jnp.exp(s[:, msl] - jnp.max(s[:, msl], axis=-1, keepdims=True))
            p_ref[rows, msl] = (e / jnp.sum(e, axis=-1, keepdims=True)).astype(BF16)
    h2 = [h + _dot(p_ref[rows, :], vwo_ref[:, :D_MODEL]) for rows, h in zip(groups, h1)]
    n3 = [_rms(h, gf_ref[...]).astype(BF16) for h in h2]
    for cb in range(D_FF // ff_block):
        for rows, n in zip(groups, n3):
            gate = _dot(n, wfi_ref[:, cb * ff_block:(cb + 1) * ff_block])
            up = _dot(n, wfi_ref[:, D_FF + cb * ff_block:D_FF + (cb + 1) * ff_block])
            act_ref[rows, cb * ff_block:(cb + 1) * ff_block] = (_silu(gate) * up).astype(BF16)
    for rows, h in zip(groups, h2):
        h3 = h + _dot(act_ref[rows, :], wfo_ref[:, :D_MODEL])
        out_ref[rows, :] = _rms(h3, gfin_ref[...])


IN_BOUNDS = tuple(sum(IN_SIZES[:i]) for i in range(len(IN_SIZES) + 1))
IN_RUNS = ((OFF_Z, IN_BOUNDS[0], IN_BOUNDS[2] - IN_BOUNDS[0]),
           (OFF_Q, IN_BOUNDS[3], IN_BOUNDS[7] - IN_BOUNDS[3]),
           (OFF_GS, IN_BOUNDS[8], IN_BOUNDS[10] - IN_BOUNDS[8]))
IN_DT_COL = IN_BOUNDS[2]
IN_A1_COL = IN_BOUNDS[7]
assert all(dst % PREP_SLAB == 0 and width % PREP_SLAB == 0 for dst, _, width in IN_RUNS)
assert OFF_SMALL == (PREP_STEPS - 1) * PREP_SLAB and IN_WIDTH_PADDED == PREP_STEPS * PREP_SLAB
assert _pitch(IN_WIDTH_ALLOC) == IN_WIDTH_ALLOC


def _prep_src_row(i):
    unit = SSD_HEADS
    start = i * (PREP_SLAB // unit)
    prev_shift = 0
    for dst, src, _ in IN_RUNS:
        start = start + jnp.where(i >= dst // PREP_SLAB, (src - dst - prev_shift) // unit, 0)
        prev_shift = src - dst
    last = (IN_BOUNDS[-1] - PREP_SLAB) // unit
    return jnp.minimum(start, last) * unit


def _prep_kernel(wint_ref, dt_ref, a1_ref, *refs):
    n_plain = (len(refs) - 1) // 2
    plain_in, owin_ref, plain_out = refs[:n_plain], refs[n_plain], refs[n_plain + 1:]
    i = pl.program_id(0)

    @pl.when(i < PREP_STEPS - 1)
    def _():
        owin_ref[...] = wint_ref[...].T.astype(BF16)

    @pl.when(i == PREP_STEPS - 1)
    def _():
        pad = jnp.zeros((LANES - SSD_HEADS - GLA_GATE_RANK, D_MODEL), F32)
        small_t = jnp.concatenate([dt_ref[...], a1_ref[...], pad], axis=0)
        owin_ref[:, 0:LANES] = small_t.T.astype(BF16)
        owin_ref[:, LANES:] = jnp.zeros((D_MODEL, PREP_SLAB - LANES), BF16)

    @pl.when(i == PREP_STEPS)
    def _():
        owin_ref[...] = jnp.zeros(owin_ref.shape, BF16)

    for src_ref, dst_ref in zip(plain_in, plain_out):
        cols = src_ref.shape[1]
        dst_ref[:, 0:cols] = src_ref[...].astype(BF16)
        if dst_ref.shape[1] > cols:
            dst_ref[:, cols:] = jnp.zeros((dst_ref.shape[0], dst_ref.shape[1] - cols), BF16)


def _resident(shape):
    return pl.BlockSpec(shape, lambda b, s: (0,) * len(shape), pipeline_mode=pl.Buffered(1))


def _row(v, width=None):
    v = v.reshape(1, -1).astype(F32)
    if width is not None and v.shape[1] < width:
        v = jnp.pad(v, ((0, 0), (0, width - v.shape[1])))
    return v


@functools.partial(jax.jit, static_argnames=("tile", "tail_tile"))
def _forward(x, mem, norm_mix, w_in, ssd_conv_w, ssd_conv_b, ssd_dt_bias, ssd_A_log, ssd_D, ssd_norm,
             gla_w_a2, gla_b_a, gla_norm, w_up_ssd, w_up_gla, w_o, norm_xattn, norm_mem, w_xq, w_xkv,
             w_xo, norm_ffn, w_ffn_in, w_ffn_out, norm_final, *, tile, tail_tile):
    batch, seq, _ = x.shape
    mem_len = mem.shape[1]
    assert seq % tile == 0 and tile % CHUNK == 0
    grid = (batch, seq // tile)

    w_in_t = jnp.swapaxes(w_in[0], 0, 1)
    plain = [w_up_ssd, w_up_gla, w_o, w_ffn_in, w_ffn_out]
    row_of = lambda i: jnp.minimum(i, PREP_STEPS - 1)
    row_block_in = lambda a: pl.BlockSpec((None, a.shape[1] // PREP_STEPS, a.shape[2]),
                                          lambda i: (0, row_of(i), 0))
    row_block_out = lambda a: pl.BlockSpec((a.shape[0] // PREP_STEPS, a.shape[1]),
                                           lambda i: (row_of(i), 0))
    rows_at = lambda n, start: pl.BlockSpec((pl.Element(n), pl.Element(D_MODEL)),
                                            lambda i: (start(i), 0))
    outs = [jax.ShapeDtypeStruct((a.shape[1], _pitch(a.shape[2])), BF16) for a in plain]
    w_in_r, w_up_ssd_b, w_up_gla_b, w_o_b, w_ffn_in_b, w_ffn_out_b = pl.pallas_call(
        _prep_kernel,
        grid=(PREP_STEPS + 1,),
        in_specs=([rows_at(PREP_SLAB, _prep_src_row),
                   rows_at(SSD_HEADS, lambda i: IN_DT_COL), rows_at(GLA_GATE_RANK, lambda i: IN_A1_COL)]
                  + [row_block_in(a) for a in plain]),
        out_specs=([pl.BlockSpec((D_MODEL, PREP_SLAB), lambda i: (0, i))]
                   + [row_block_out(a) for a in outs]),
        out_shape=[jax.ShapeDtypeStruct((D_MODEL, IN_WIDTH_ALLOC), BF16)] + outs,
        compiler_params=pltpu.CompilerParams(
            dimension_semantics=("arbitrary",), vmem_limit_bytes=VMEM_LIMIT_BYTES),
        name="prep",
    )(w_in_t, w_in_t, w_in_t, *plain)
    wa2 = jnp.zeros((LANES, GLA_DK), F32).at[SMALL_A1:SMALL_A1 + GLA_GATE_RANK].set(gla_w_a2[0]).astype(BF16)
    dskip = jnp.broadcast_to(ssd_D[0].astype(F32)[:, None], (SSD_HEADS, tile))

    tok_spec = pl.BlockSpec((None, tile, D_MODEL), lambda b, s: (b, s, 0))

    mixer_inputs = [
        (x, tok_spec),
        (_row(norm_mix[0]), _resident((1, D_MODEL))),
        (w_in_r, _resident((D_MODEL, IN_WIDTH_ALLOC))),
        (ssd_conv_w[0].reshape(SSD_CONV, SSD_CONV_CH), _resident((SSD_CONV, SSD_CONV_CH))),
        (_row(ssd_conv_b[0]), _resident((1, SSD_CONV_CH))),
        (_row(ssd_dt_bias[0], LANES), _resident((1, LANES))),
        (_row(ssd_A_log[0], LANES), _resident((1, LANES))),
        (dskip, _resident((SSD_HEADS, tile))),
        (_row(ssd_norm[0]), _resident((1, SSD_D_INNER))),
        (wa2, _resident((LANES, GLA_DK))),
        (_row(gla_b_a[0]), _resident((1, GLA_DK))),
        (_row(gla_norm[0]), _resident((1, GLA_HEAD_V))),
        (w_up_ssd_b, _resident(w_up_ssd_b.shape)),
        (w_up_gla_b, _resident(w_up_gla_b.shape)),
        (w_o_b, _resident(w_o_b.shape)),
    ]
    mixer_scratch = [
        pltpu.VMEM((tile, SSD_D_INNER), F32),
        pltpu.VMEM((CONV_SLABS, tile + CONV_HALO, LANES), F32),
        pltpu.VMEM((tile, GLA_DK), F32),
        pltpu.VMEM((tile, GLA_DK), F32),
        pltpu.VMEM((tile, GLA_DV), BF16),
        pltpu.VMEM((tile, GLA_DV), F32),
        pltpu.VMEM((tile, D_MODEL), F32),
        pltpu.VMEM((tile, D_MODEL), F32),
        pltpu.VMEM((CONV_SLABS, tile, LANES), F32),
        pltpu.VMEM((tile, SSD_GROUPS * SSD_STATE), BF16),
        pltpu.VMEM((tile, SSD_GROUPS * SSD_STATE), BF16),
        pltpu.VMEM((tile, LANES), F32),
        pltpu.VMEM((tile, LANES), F32),
        pltpu.VMEM((tile, GLA_DK), F32),
        pltpu.VMEM((SSD_D_INNER, tile), F32),
        pltpu.VMEM((SSD_HEADS_PER_GROUP * SSD_HEAD_DIM, tile), BF16),
        pltpu.VMEM((tile, SSD_D_INNER), BF16),
        pltpu.VMEM((tile, GLA_DK), BF16),
        pltpu.VMEM((tile, GLA_DK), BF16),
        pltpu.VMEM((tile, GLA_DK), BF16),
        pltpu.VMEM((tile // CHUNK * GLA_HEADS, CHUNK, CHUNK), BF16),
        pltpu.VMEM((tile // CHUNK, GLA_DK, GLA_HEAD_V), BF16),
        pltpu.VMEM((tile, GLA_DV), F32),
        pltpu.VMEM((tile, GLA_DV), BF16),
        pltpu.VMEM((SSD_D_INNER, SSD_STATE), F32),
        pltpu.VMEM((GLA_DK, GLA_HEAD_V), F32),
    ]
    h1 = pl.pallas_call(
        functools.partial(_mixer_kernel, tile=tile),
        grid=grid,
        in_specs=[spec for _, spec in mixer_inputs],
        out_specs=tok_spec,
        out_shape=jax.ShapeDtypeStruct(x.shape, F32),
        scratch_shapes=mixer_scratch,
        compiler_params=pltpu.CompilerParams(
            dimension_semantics=("arbitrary", "arbitrary"), vmem_limit_bytes=VMEM_LIMIT_BYTES),
        name="mixer",
    )(*[a for a, _ in mixer_inputs])

    att_width = XATTN_HEADS * mem_len
    wqk_shape = (D_MODEL, _pitch(att_width))
    vwo_shape = (att_width, _pitch(D_MODEL))
    mem_spec = pl.BlockSpec((None, mem_len, D_MODEL), lambda b: (b, 0, 0))
    whole = lambda shape: pl.BlockSpec(shape, lambda b: (0,) * len(shape))
    layer0 = lambda a: pl.BlockSpec((None,) + a.shape[1:], lambda b: (0, 0, 0),
                                    pipeline_mode=pl.Buffered(1))
    wqk, vwo = pl.pallas_call(
        _memkv_kernel,
        grid=(batch,),
        in_specs=[mem_spec, whole((1, D_MODEL)), layer0(w_xkv), layer0(w_xq), layer0(w_xo)],
        out_specs=[pl.BlockSpec((None,) + wqk_shape, lambda b: (b, 0, 0)),
                   pl.BlockSpec((None,) + vwo_shape, lambda b: (b, 0, 0))],
        out_shape=[jax.ShapeDtypeStruct((batch,) + wqk_shape, BF16),
                   jax.ShapeDtypeStruct((batch,) + vwo_shape, BF16)],
        compiler_params=pltpu.CompilerParams(
            dimension_semantics=("arbitrary",), vmem_limit_bytes=VMEM_LIMIT_BYTES),
        name="memkv",
    )(mem, _row(norm_mem[0]), w_xkv, w_xq, w_xo)

    assert seq % tail_tile == 0
    tail_spec = pl.BlockSpec((None, tail_tile, D_MODEL), lambda b, s: (b, s, 0))
    out = pl.pallas_call(
        _tail_kernel,
        grid=(batch, seq // tail_tile),
        in_specs=[tail_spec,
                  _resident((1, D_MODEL)),
                  pl.BlockSpec((None,) + wqk_shape, lambda b, s: (b, 0, 0)),
                  pl.BlockSpec((None,) + vwo_shape, lambda b, s: (b, 0, 0)),
                  _resident((1, D_MODEL)),
                  _resident(w_ffn_in_b.shape),
                  _resident(w_ffn_out_b.shape),
                  _resident((1, D_MODEL))],
        out_specs=tail_spec,
        out_shape=jax.ShapeDtypeStruct(x.shape, F32),
        scratch_shapes=[pltpu.VMEM((tail_tile, att_width), BF16),
                        pltpu.VMEM((tail_tile, D_FF), BF16)],
        compiler_params=pltpu.CompilerParams(
            dimension_semantics=("arbitrary", "arbitrary"), vmem_limit_bytes=VMEM_LIMIT_BYTES),
        name="tail",
    )(h1, _row(norm_xattn[0]), wqk, vwo, _row(norm_ffn[0]), w_ffn_in_b, w_ffn_out_b, _row(norm_final))
    return out


def kernel(x, mem, norm_mix, w_in, ssd_conv_w, ssd_conv_b, ssd_dt_bias, ssd_A_log, ssd_D, ssd_norm,
           gla_w_a2, gla_b_a, gla_norm, w_up_ssd, w_up_gla, w_o, norm_xattn, norm_mem, w_xq, w_xkv,
           w_xo, norm_ffn, w_ffn_in, w_ffn_out, norm_final):
    return _forward(x, mem, norm_mix, w_in, ssd_conv_w, ssd_conv_b, ssd_dt_bias, ssd_A_log, ssd_D,
                    ssd_norm, gla_w_a2, gla_b_a, gla_norm, w_up_ssd, w_up_gla, w_o, norm_xattn,
                    norm_mem, w_xq, w_xkv, w_xo, norm_ffn, w_ffn_in, w_ffn_out, norm_final,
                    tile=MIXER_TILE, tail_tile=TAIL_TILE)
```

```python
import functools

import jax
import jax.numpy as jnp
from jax import lax
from jax.experimental import pallas as pl
from jax.experimental.pallas import tpu as pltpu

F32 = jnp.float32
BF16 = jnp.bfloat16

D_MODEL = 1024
EPS = 1e-6
LOG2_E = 1.4426950408889634
CHUNK = 64
SSD_D_INNER = 1024
SSD_HEAD_DIM = 64
SSD_HEADS = 16
SSD_GROUPS = 2
SSD_HEADS_PER_GROUP = SSD_HEADS // SSD_GROUPS
SSD_STATE = 128
SSD_CONV = 4
SSD_CONV_CH = SSD_D_INNER + 2 * SSD_GROUPS * SSD_STATE
GLA_HEADS = 4
GLA_DK = 512
GLA_DV = 1024
GLA_HEAD_K = GLA_DK // GLA_HEADS
GLA_HEAD_V = GLA_DV // GLA_HEADS
GLA_GATE_RANK = 16
GLA_TAU = 16.0
XATTN_HEADS = 4
XATTN_HEAD_DIM = D_MODEL // XATTN_HEADS
D_FF = 2816
IN_SIZES = (SSD_D_INNER, SSD_CONV_CH, SSD_HEADS, GLA_DK, GLA_DK, GLA_DV, GLA_DV, GLA_GATE_RANK,
            D_MODEL, D_MODEL)

LANES = 128
SUBLANES = 8
VMEM_LIMIT_BYTES = 56 * 1024 * 1024
MIXER_TILE = 256
GLA_FIRST_SSD_HEAD = 2
TAIL_TILE = 1024
TAIL_ROW_GROUPS = 4
PREP_STEPS = 16
PREP_SLAB = 512

OFF_Z = 0
OFF_XBC = OFF_Z + SSD_D_INNER
OFF_Q = OFF_XBC + SSD_CONV_CH
OFF_K = OFF_Q + GLA_DK
OFF_V = OFF_K + GLA_DK
OFF_R = OFF_V + GLA_DV
OFF_GS = OFF_R + GLA_DV
OFF_GG = OFF_GS + D_MODEL
OFF_SMALL = OFF_GG + D_MODEL
IN_WIDTH_PADDED = OFF_SMALL + PREP_SLAB
IN_WIDTH_ALLOC = IN_WIDTH_PADDED + PREP_SLAB
SMALL_DT = 0
SMALL_A1 = SSD_HEADS

CONV_HALO = SUBLANES
CONV_SLABS = SSD_CONV_CH // LANES
CONV_PHASES = 4
assert SSD_STATE == LANES and LANES % SSD_HEAD_DIM == 0


def _pitch(width):
    return width + LANES if (width // LANES) % SUBLANES == 0 else width


def _dot(a, b):
    return jnp.dot(a, b, preferred_element_type=F32)


def _dot_nt(a, b):
    return lax.dot_general(a, b, (((1,), (1,)), ((), ())), preferred_element_type=F32)


def _dot_tn(a, b):
    return lax.dot_general(a, b, (((0,), (0,)), ((), ())), preferred_element_type=F32)


def _rms(x, g):
    return x * lax.rsqrt(jnp.mean(x * x, axis=-1, keepdims=True) + EPS) * g


def _silu(x):
    return x * jax.nn.sigmoid(x)


def _split3(x):
    hi = x.astype(BF16)
    r1 = x - hi.astype(F32)
    mid = r1.astype(BF16)
    lo = (r1 - mid.astype(F32)).astype(BF16)
    return hi, mid, lo


def _cumsum_rows(tri, x):
    hi, mid, lo = _split3(x)
    return _dot(tri, hi) + _dot(tri, mid) + _dot(tri, lo)


def _cumsum_lanes(x, upper):
    hi, mid, lo = _split3(x)
    return _dot(hi, upper) + _dot(mid, upper) + _dot(lo, upper)


def _mixer_kernel(x_ref, gmix_ref, win_ref, convw_ref, convb_ref, dtb_ref, alog_ref, dskip_ref,
                  ssdnorm_ref, wa2_ref, ba_ref, glanorm_ref, wus_ref, wug_ref, wo_ref,
                  h_ref,
                  z_ref, xpad_ref, q_ref, k_ref, v_ref, r_ref, gs_ref, gg_ref,
                  xbc_ref, b_ref, c_ref, dt_ref, a_ref, la_ref, yt_ref, xd_ref, ys_ref,
                  qt_ref, kt_ref, kh_ref, att_ref, sb_ref, o_ref, yg_ref,
                  sstate_ref, gstate_ref, *, tile):
    s = pl.program_id(1)

    @pl.when(s == 0)
    def _():
        sstate_ref[...] = jnp.zeros_like(sstate_ref)
        gstate_ref[...] = jnp.zeros_like(gstate_ref)
        xpad_ref[:, 0:CONV_HALO, :] = jnp.zeros((CONV_SLABS, CONV_HALO, LANES), F32)

    n = _rms(x_ref[...], gmix_ref[...]).astype(BF16)

    def proj(off, width):
        return _dot(n, win_ref[:, off:off + width])

    slab = 512
    deferred = [(dst, off, c0)
                for dst, off, width in ((q_ref, OFF_Q, GLA_DK), (k_ref, OFF_K, GLA_DK),
                                        (v_ref, OFF_V, GLA_DV), (r_ref, OFF_R, GLA_DV),
                                        (z_ref, OFF_Z, SSD_D_INNER), (gs_ref, OFF_GS, D_MODEL),
                                        (gg_ref, OFF_GG, D_MODEL))
                for c0 in range(0, width, slab)]

    def emit_proj(count=1):
        for _ in range(count):
            if deferred:
                dst, off, c0 = deferred.pop(0)
                dst[:, c0:c0 + slab] = proj(off + c0, slab).astype(dst.dtype)

    for c0 in range(0, SSD_CONV_CH, slab):
        xbc = proj(OFF_XBC + c0, slab)
        for t in range(slab // LANES):
            xpad_ref[c0 // LANES + t, CONV_HALO:CONV_HALO + tile, :] = xbc[:, t * LANES:(t + 1) * LANES]
    small = proj(OFF_SMALL, LANES)

    dt = jax.nn.softplus(small + dtb_ref[...])
    dt_ref[...] = dt
    a_ref[...] = dt * (-jnp.exp(alog_ref[...]))
    logits = _dot(small.astype(BF16), wa2_ref[...]) + ba_ref[...]
    la_ref[...] = jax.nn.log_sigmoid(logits) / GLA_TAU

    rows_per_phase = tile // CONV_PHASES
    first = CONV_HALO - (SSD_CONV - 1)
    for sl in range(CONV_SLABS):
        cols = slice(sl * LANES, (sl + 1) * LANES)
        taps = {d: xpad_ref[sl, pl.ds(first + d, rows_per_phase, stride=CONV_PHASES), :]
                for d in range(CONV_PHASES + SSD_CONV - 1)}
        for phase in range(CONV_PHASES):
            acc = convb_ref[:, cols]
            for j in range(SSD_CONV):
                acc = acc + convw_ref[j:j + 1, cols] * taps[phase + j]
            xbc_ref[sl, pl.ds(phase, rows_per_phase, stride=CONV_PHASES), :] = _silu(acc)
        if sl % 2 == 1:
            emit_proj()
    xpad_ref[:, 0:CONV_HALO, :] = xpad_ref[:, tile:tile + CONV_HALO, :]
    x_slabs = SSD_D_INNER // LANES
    for g in range(SSD_GROUPS):
        b_ref[:, g * SSD_STATE:(g + 1) * SSD_STATE] = xbc_ref[x_slabs + g].astype(BF16)
        c_ref[:, g * SSD_STATE:(g + 1) * SSD_STATE] = xbc_ref[x_slabs + SSD_GROUPS + g].astype(BF16)

    rid = lax.broadcasted_iota(jnp.int32, (tile, tile), 0)
    cid = lax.broadcasted_iota(jnp.int32, (tile, tile), 1)

    def gla_stages():
        n_chunks = tile // CHUNK
        chunk_of = lambda idx: lax.shift_right_logical(idx, CHUNK.bit_length() - 1)
        tri_blocks = ((rid >= cid) & (chunk_of(rid) == chunk_of(cid))).astype(BF16)
        bcum = _cumsum_rows(tri_blocks, la_ref[...]) * LOG2_E
        blast = jnp.concatenate(
            [jnp.broadcast_to(bcum[(i + 1) * CHUNK - 1:(i + 1) * CHUNK, :], (CHUNK, GLA_DK))
             for i in range(n_chunks)], axis=0)
        kk = k_ref[...]
        yield
        qt_ref[...] = (q_ref[...] * (GLA_HEAD_K ** -0.5) * jnp.exp2(bcum)).astype(BF16)
        yield
        kt_ref[...] = (kk * jnp.exp2(-bcum)).astype(BF16)
        yield
        kh_ref[...] = (kk * jnp.exp2(blast - bcum)).astype(BF16)
        dec_t = jnp.exp2(blast).T
        causal = (lax.broadcasted_iota(jnp.int32, (CHUNK, CHUNK), 0)
                  >= lax.broadcasted_iota(jnp.int32, (CHUNK, CHUNK), 1))
        ksls = [slice(j * GLA_HEAD_K, (j + 1) * GLA_HEAD_K) for j in range(GLA_HEADS)]
        vsls = [slice(j * GLA_HEAD_V, (j + 1) * GLA_HEAD_V) for j in range(GLA_HEADS)]
        for i in range(n_chunks):
            rs = slice(i * CHUNK, (i + 1) * CHUNK)
            for j in range(GLA_HEADS):
                scores = _dot_nt(qt_ref[rs, ksls[j]], kt_ref[rs, ksls[j]])
                att_ref[i * GLA_HEADS + j] = jnp.where(causal, scores, 0.0).astype(BF16)
            yield
        states = [gstate_ref[ksl, :] for ksl in ksls]
        for i in range(n_chunks):
            rs = slice(i * CHUNK, (i + 1) * CHUNK)
            for j in range(GLA_HEADS):
                sb_ref[i, ksls[j], :] = states[j].astype(BF16)
                update = _dot_tn(kh_ref[rs, ksls[j]], v_ref[rs, vsls[j]])
                states[j] = states[j] * dec_t[ksls[j], i * CHUNK:i * CHUNK + 1] + update
            yield
        for j in range(GLA_HEADS):
            gstate_ref[ksls[j], :] = states[j]
        for i in range(n_chunks):
            rs = slice(i * CHUNK, (i + 1) * CHUNK)
            for j in range(GLA_HEADS):
                o_ref[rs, vsls[j]] = (_dot(att_ref[i * GLA_HEADS + j], v_ref[rs, vsls[j]])
                                      + _dot(qt_ref[rs, ksls[j]], sb_ref[i, ksls[j], :]))
            yield
        for j in range(GLA_HEADS):
            yg_ref[:, vsls[j]] = (_rms(o_ref[:, vsls[j]], glanorm_ref[...])
                                  * _silu(r_ref[:, vsls[j]])).astype(BF16)
            yield

    gla = gla_stages()

    upper = rid <= cid
    a_t = a_ref[...].T
    acs_t = _cumsum_lanes(a_t, upper.astype(BF16)) * LOG2_E
    acs = acs_t.T
    dt_t = dt_ref[...].T
    xs_t = [xbc_ref[sl].T for sl in range(x_slabs)]
    heads_per_slab = LANES // SSD_HEAD_DIM
    hp = SSD_HEADS_PER_GROUP * SSD_HEAD_DIM
    for g in range(SSD_GROUPS):
        nsl = slice(g * SSD_STATE, (g + 1) * SSD_STATE)
        bg = b_ref[:, nsl]
        cg = c_ref[:, nsl]
        cb_t = jnp.where(upper, _dot_nt(bg, cg), 0.0)
        state = sstate_ref[g * hp:(g + 1) * hp, :]
        y_off_t = _dot_nt(state.astype(BF16), cg)
        for r in range(SSD_HEADS_PER_GROUP):
            h = g * SSD_HEADS_PER_GROUP + r
            psl = slice(h * SSD_HEAD_DIM, (h + 1) * SSD_HEAD_DIM)
            rsl = slice(r * SSD_HEAD_DIM, (r + 1) * SSD_HEAD_DIM)
            col = acs[:, h:h + 1]
            row = acs_t[h:h + 1, :]
            last = row[:, tile - 1:tile]
            m_t = (jnp.exp2(jnp.minimum(row - col, 0.0)) * cb_t).astype(BF16)
            in_slab = (h % heads_per_slab) * SSD_HEAD_DIM
            xh_t = xs_t[h // heads_per_slab][in_slab:in_slab + SSD_HEAD_DIM, :]
            xdt_t = xh_t * dt_t[h:h + 1, :]
            y_diag_t = _dot(xdt_t.astype(BF16), m_t)
            yt_ref[psl, :] = (y_diag_t + y_off_t[rsl, :] * jnp.exp2(row)
                              + dskip_ref[h:h + 1, :] * xh_t)
            xd_ref[rsl, :] = (xdt_t * jnp.exp2(last - row)).astype(BF16)
            sstate_ref[psl, :] = state[rsl, :] * jnp.exp2(last)
            emit_proj()
            if h >= GLA_FIRST_SSD_HEAD:
                next(gla, None)
        sstate_ref[g * hp:(g + 1) * hp, :] += _dot(xd_ref[...], bg)
    emit_proj(len(deferred))
    yz = yt_ref[...].T * _silu(z_ref[...])
    gn = SSD_D_INNER // SSD_GROUPS
    for g in range(SSD_GROUPS):
        csl = slice(g * gn, (g + 1) * gn)
        ys_ref[:, csl] = _rms(yz[:, csl], ssdnorm_ref[:, csl]).astype(BF16)
    merged_ssd = jax.nn.sigmoid(gs_ref[...]) * _dot(ys_ref[...], wus_ref[:, :D_MODEL])

    for _ in gla:
        pass

    merged = merged_ssd + jax.nn.sigmoid(gg_ref[...]) * _dot(yg_ref[...], wug_ref[:, :D_MODEL])
    h_ref[...] = x_ref[...] + _dot(merged.astype(BF16), wo_ref[:, :D_MODEL])


def _memkv_kernel(mem_ref, g_ref, wkv_ref, wq_ref, wo_ref, wqk_ref, vwo_ref):
    mem_len = mem_ref.shape[0]
    m = _rms(mem_ref[...], g_ref[...]).astype(BF16)
    kv = _dot(m, wkv_ref[...].astype(BF16))
    for j in range(XATTN_HEADS):
        dsl = slice(j * XATTN_HEAD_DIM, (j + 1) * XATTN_HEAD_DIM)
        msl = slice(j * mem_len, (j + 1) * mem_len)
        k_j = kv[:, dsl].astype(BF16)
        v_j = kv[:, D_MODEL + j * XATTN_HEAD_DIM:D_MODEL + (j + 1) * XATTN_HEAD_DIM].astype(BF16)
        wq_j = wq_ref[:, dsl].astype(BF16)
        wqk_ref[:, msl] = (_dot_nt(wq_j, k_j) * (XATTN_HEAD_DIM ** -0.5)).astype(BF16)
        vwo_ref[msl, 0:D_MODEL] = _dot(v_j, wo_ref[dsl, :].astype(BF16)).astype(BF16)
    att_width = XATTN_HEADS * mem_len
    for ref, used in ((wqk_ref, att_width), (vwo_ref, D_MODEL)):
        if ref.shape[1] > used:
            ref[:, used:] = jnp.zeros((ref.shape[0], ref.shape[1] - used), BF16)


def _tail_kernel(h_ref, gx_ref, wqk_ref, vwo_ref, gf_ref, wfi_ref, wfo_ref, gfin_ref,
                 out_ref, p_ref, act_ref):
    att_width = p_ref.shape[1]
    mem_len = att_width // XATTN_HEADS
    ff_block = D_FF // 2
    rows_per_group = h_ref.shape[0] // TAIL_ROW_GROUPS
    groups = [slice(g * rows_per_group, (g + 1) * rows_per_group) for g in range(TAIL_ROW_GROUPS)]
    h1 = [h_ref[rows, :] for rows in groups]
    sc = [_dot(_rms(h, gx_ref[...]).astype(BF16), wqk_ref[:, :att_width]) for h in h1]
    for rows, s in zip(groups, sc):
        for j in range(XATTN_HEADS):
            msl = slice(j * mem_len, (j + 1) * mem_len)
            e = jnp.exp(s[:, msl] - jnp.max(s[:, msl], axis=-1, keepdims=True))
            p_ref[rows, msl] = (e / jnp.sum(e, axis=-1, keepdims=True)).astype(BF16)
    h2 = [h + _dot(p_ref[rows, :], vwo_ref[:, :D_MODEL]) for rows, h in zip(groups, h1)]
    n3 = [_rms(h, gf_ref[...]).astype(BF16) for h in h2]
    for cb in range(D_FF // ff_block):
        for rows, n in zip(groups, n3):
            gate = _dot(n, wfi_ref[:, cb * ff_block:(cb + 1) * ff_block])
            up = _dot(n, wfi_ref[:, D_FF + cb * ff_block:D_FF + (cb + 1) * ff_block])
            act_ref[rows, cb * ff_block:(cb + 1) * ff_block] = (_silu(gate) * up).astype(BF16)
    for rows, h in zip(groups, h2):
        h3 = h + _dot(act_ref[rows, :], wfo_ref[:, :D_MODEL])
        out_ref[rows, :] = _rms(h3, gfin_ref[...])


IN_BOUNDS = tuple(sum(IN_SIZES[:i]) for i in range(len(IN_SIZES) + 1))
IN_RUNS = ((OFF_Z, IN_BOUNDS[0], IN_BOUNDS[2] - IN_BOUNDS[0]),
           (OFF_Q, IN_BOUNDS[3], IN_BOUNDS[7] - IN_BOUNDS[3]),
           (OFF_GS, IN_BOUNDS[8], IN_BOUNDS[10] - IN_BOUNDS[8]))
IN_DT_COL = IN_BOUNDS[2]
IN_A1_COL = IN_BOUNDS[7]
assert all(dst % PREP_SLAB == 0 and width % PREP_SLAB == 0 for dst, _, width in IN_RUNS)
assert OFF_SMALL == (PREP_STEPS - 1) * PREP_SLAB and IN_WIDTH_PADDED == PREP_STEPS * PREP_SLAB
assert _pitch(IN_WIDTH_ALLOC) == IN_WIDTH_ALLOC


def _prep_src_row(i):
    unit = SSD_HEADS
    start = i * (PREP_SLAB // unit)
    prev_shift = 0
    for dst, src, _ in IN_RUNS:
        start = start + jnp.where(i >= dst // PREP_SLAB, (src - dst - prev_shift) // unit, 0)
        prev_shift = src - dst
    last = (IN_BOUNDS[-1] - PREP_SLAB) // unit
    return jnp.minimum(start, last) * unit


def _prep_kernel(wint_ref, dt_ref, a1_ref, *refs):
    n_plain = (len(refs) - 1) // 2
    plain_in, owin_ref, plain_out = refs[:n_plain], refs[n_plain], refs[n_plain + 1:]
    i = pl.program_id(0)

    @pl.when(i < PREP_STEPS - 1)
    def _():
        owin_ref[...] = wint_ref[...].T.astype(BF16)

    @pl.when(i == PREP_STEPS - 1)
    def _():
        pad = jnp.zeros((LANES - SSD_HEADS - GLA_GATE_RANK, D_MODEL), F32)
        small_t = jnp.concatenate([dt_ref[...], a1_ref[...], pad], axis=0)
        owin_ref[:, 0:LANES] = small_t.T.astype(BF16)
        owin_ref[:, LANES:] = jnp.zeros((D_MODEL, PREP_SLAB - LANES), BF16)

    @pl.when(i == PREP_STEPS)
    def _():
        owin_ref[...] = jnp.zeros(owin_ref.shape, BF16)

    for src_ref, dst_ref in zip(plain_in, plain_out):
        cols = src_ref.shape[1]
        dst_ref[:, 0:cols] = src_ref[...].astype(BF16)
        if dst_ref.shape[1] > cols:
            dst_ref[:, cols:] = jnp.zeros((dst_ref.shape[0], dst_ref.shape[1] - cols), BF16)


def _resident(shape):
    return pl.BlockSpec(shape, lambda b, s: (0,) * len(shape), pipeline_mode=pl.Buffered(1))


def _row(v, width=None):
    v = v.reshape(1, -1).astype(F32)
    if width is not None and v.shape[1] < width:
        v = jnp.pad(v, ((0, 0), (0, width - v.shape[1])))
    return v


@functools.partial(jax.jit, static_argnames=("tile", "tail_tile"))
def _forward(x, mem, norm_mix, w_in, ssd_conv_w, ssd_conv_b, ssd_dt_bias, ssd_A_log, ssd_D, ssd_norm,
             gla_w_a2, gla_b_a, gla_norm, w_up_ssd, w_up_gla, w_o, norm_xattn, norm_mem, w_xq, w_xkv,
             w_xo, norm_ffn, w_ffn_in, w_ffn_out, norm_final, *, tile, tail_tile):
    batch, seq, _ = x.shape
    mem_len = mem.shape[1]
    assert seq % tile == 0 and tile % CHUNK == 0
    grid = (batch, seq // tile)

    w_in_t = jnp.swapaxes(w_in[0], 0, 1)
    plain = [w_up_ssd, w_up_gla, w_o, w_ffn_in, w_ffn_out]
    row_of = lambda i: jnp.minimum(i, PREP_STEPS - 1)
    row_block_in = lambda a: pl.BlockSpec((None, a.shape[1] // PREP_STEPS, a.shape[2]),
                                          lambda i: (0, row_of(i), 0))
    row_block_out = lambda a: pl.BlockSpec((a.shape[0] // PREP_STEPS, a.shape[1]),
                                           lambda i: (row_of(i), 0))
    rows_at = lambda n, start: pl.BlockSpec((pl.Element(n), pl.Element(D_MODEL)),
                                            lambda i: (start(i), 0))
    outs = [jax.ShapeDtypeStruct((a.shape[1], _pitch(a.shape[2])), BF16) for a in plain]
    w_in_r, w_up_ssd_b, w_up_gla_b, w_o_b, w_ffn_in_b, w_ffn_out_b = pl.pallas_call(
        _prep_kernel,
        grid=(PREP_STEPS + 1,),
        in_specs=([rows_at(PREP_SLAB, _prep_src_row),
                   rows_at(SSD_HEADS, lambda i: IN_DT_COL), rows_at(GLA_GATE_RANK, lambda i: IN_A1_COL)]
                  + [row_block_in(a) for a in plain]),
        out_specs=([pl.BlockSpec((D_MODEL, PREP_SLAB), lambda i: (0, i))]
                   + [row_block_out(a) for a in outs]),
        out_shape=[jax.ShapeDtypeStruct((D_MODEL, IN_WIDTH_ALLOC), BF16)] + outs,
        compiler_params=pltpu.CompilerParams(
            dimension_semantics=("arbitrary",), vmem_limit_bytes=VMEM_LIMIT_BYTES),
        name="prep",
    )(w_in_t, w_in_t, w_in_t, *plain)
    wa2 = jnp.zeros((LANES, GLA_DK), F32).at[SMALL_A1:SMALL_A1 + GLA_GATE_RANK].set(gla_w_a2[0]).astype(BF16)
    dskip = jnp.broadcast_to(ssd_D[0].astype(F32)[:, None], (SSD_HEADS, tile))

    tok_spec = pl.BlockSpec((None, tile, D_MODEL), lambda b, s: (b, s, 0))

    mixer_inputs = [
        (x, tok_spec),
        (_row(norm_mix[0]), _resident((1, D_MODEL))),
        (w_in_r, _resident((D_MODEL, IN_WIDTH_ALLOC))),
        (ssd_conv_w[0].reshape(SSD_CONV, SSD_CONV_CH), _resident((SSD_CONV, SSD_CONV_CH))),
        (_row(ssd_conv_b[0]), _resident((1, SSD_CONV_CH))),
        (_row(ssd_dt_bias[0], LANES), _resident((1, LANES))),
        (_row(ssd_A_log[0], LANES), _resident((1, LANES))),
        (dskip, _resident((SSD_HEADS, tile))),
        (_row(ssd_norm[0]), _resident((1, SSD_D_INNER))),
        (wa2, _resident((LANES, GLA_DK))),
        (_row(gla_b_a[0]), _resident((1, GLA_DK))),
        (_row(gla_norm[0]), _resident((1, GLA_HEAD_V))),
        (w_up_ssd_b, _resident(w_up_ssd_b.shape)),
        (w_up_gla_b, _resident(w_up_gla_b.shape)),
        (w_o_b, _resident(w_o_b.shape)),
    ]
    mixer_scratch = [
        pltpu.VMEM((tile, SSD_D_INNER), F32),
        pltpu.VMEM((CONV_SLABS, tile + CONV_HALO, LANES), F32),
        pltpu.VMEM((tile, GLA_DK), F32),
        pltpu.VMEM((tile, GLA_DK), F32),
        pltpu.VMEM((tile, GLA_DV), BF16),
        pltpu.VMEM((tile, GLA_DV), F32),
        pltpu.VMEM((tile, D_MODEL), F32),
        pltpu.VMEM((tile, D_MODEL), F32),
        pltpu.VMEM((CONV_SLABS, tile, LANES), F32),
        pltpu.VMEM((tile, SSD_GROUPS * SSD_STATE), BF16),
        pltpu.VMEM((tile, SSD_GROUPS * SSD_STATE), BF16),
        pltpu.VMEM((tile, LANES), F32),
        pltpu.VMEM((tile, LANES), F32),
        pltpu.VMEM((tile, GLA_DK), F32),
        pltpu.VMEM((SSD_D_INNER, tile), F32),
        pltpu.VMEM((SSD_HEADS_PER_GROUP * SSD_HEAD_DIM, tile), BF16),
        pltpu.VMEM((tile, SSD_D_INNER), BF16),
        pltpu.VMEM((tile, GLA_DK), BF16),
        pltpu.VMEM((tile, GLA_DK), BF16),
        pltpu.VMEM((tile, GLA_DK), BF16),
        pltpu.VMEM((tile // CHUNK * GLA_HEADS, CHUNK, CHUNK), BF16),
        pltpu.VMEM((tile // CHUNK, GLA_DK, GLA_HEAD_V), BF16),
        pltpu.VMEM((tile, GLA_DV), F32),
        pltpu.VMEM((tile, GLA_DV), BF16),
        pltpu.VMEM((SSD_D_INNER, SSD_STATE), F32),
        pltpu.VMEM((GLA_DK, GLA_HEAD_V), F32),
    ]
    h1 = pl.pallas_call(
        functools.partial(_mixer_kernel, tile=tile),
        grid=grid,
        in_specs=[spec for _, spec in mixer_inputs],
        out_specs=tok_spec,
        out_shape=jax.ShapeDtypeStruct(x.shape, F32),
        scratch_shapes=mixer_scratch,
        compiler_params=pltpu.CompilerParams(
            dimension_semantics=("arbitrary", "arbitrary"), vmem_limit_bytes=VMEM_LIMIT_BYTES),
        name="mixer",
    )(*[a for a, _ in mixer_inputs])

    att_width = XATTN_HEADS * mem_len
    wqk_shape = (D_MODEL, _pitch(att_width))
    vwo_shape = (att_width, _pitch(D_MODEL))
    mem_spec = pl.BlockSpec((None, mem_len, D_MODEL), lambda b: (b, 0, 0))
    whole = lambda shape: pl.BlockSpec(shape, lambda b: (0,) * len(shape))
    layer0 = lambda a: pl.BlockSpec((None,) + a.shape[1:], lambda b: (0, 0, 0),
                                    pipeline_mode=pl.Buffered(1))
    wqk, vwo = pl.pallas_call(
        _memkv_kernel,
        grid=(batch,),
        in_specs=[mem_spec, whole((1, D_MODEL)), layer0(w_xkv), layer0(w_xq), layer0(w_xo)],
        out_specs=[pl.BlockSpec((None,) + wqk_shape, lambda b: (b, 0, 0)),
                   pl.BlockSpec((None,) + vwo_shape, lambda b: (b, 0, 0))],
        out_shape=[jax.ShapeDtypeStruct((batch,) + wqk_shape, BF16),
                   jax.ShapeDtypeStruct((batch,) + vwo_shape, BF16)],
        compiler_params=pltpu.CompilerParams(
            dimension_semantics=("arbitrary",), vmem_limit_bytes=VMEM_LIMIT_BYTES),
        name="memkv",
    )(mem, _row(norm_mem[0]), w_xkv, w_xq, w_xo)

    assert seq % tail_tile == 0
    tail_spec = pl.BlockSpec((None, tail_tile, D_MODEL), lambda b, s: (b, s, 0))
    out = pl.pallas_call(
        _tail_kernel,
        grid=(batch, seq // tail_tile),
        in_specs=[tail_spec,
                  _resident((1, D_MODEL)),
                  pl.BlockSpec((None,) + wqk_shape, lambda b, s: (b, 0, 0)),
                  pl.BlockSpec((None,) + vwo_shape, lambda b, s: (b, 0, 0)),
                  _resident((1, D_MODEL)),
                  _resident(w_ffn_in_b.shape),
                  _resident(w_ffn_out_b.shape),
                  _resident((1, D_MODEL))],
        out_specs=tail_spec,
        out_shape=jax.ShapeDtypeStruct(x.shape, F32),
        scratch_shapes=[pltpu.VMEM((tail_tile, att_width), BF16),
                        pltpu.VMEM((tail_tile, D_FF), BF16)],
        compiler_params=pltpu.CompilerParams(
            dimension_semantics=("arbitrary", "arbitrary"), vmem_limit_bytes=VMEM_LIMIT_BYTES),
        name="tail",
    )(h1, _row(norm_xattn[0]), wqk, vwo, _row(norm_ffn[0]), w_ffn_in_b, w_ffn_out_b, _row(norm_final))
    return out


def kernel(x, mem, norm_mix, w_in, ssd_conv_w, ssd_conv_b, ssd_dt_bias, ssd_A_log, ssd_D, ssd_norm,
           gla_w_a2, gla_b_a, gla_norm, w_up_ssd, w_up_gla, w_o, norm_xattn, norm_mem, w_xq, w_xkv,
           w_xo, norm_ffn, w_ffn_in, w_ffn_out, norm_final):
    return _forward(x, mem, norm_mix, w_in, ssd_conv_w, ssd_conv_b, ssd_dt_bias, ssd_A_log, ssd_D,
                    ssd_norm, gla_w_a2, gla_b_a, gla_norm, w_up_ssd, w_up_gla, w_o, norm_xattn,
                    norm_mem, w_xq, w_xkv, w_xo, norm_ffn, w_ffn_in, w_ffn_out, norm_final,
                    tile=MIXER_TILE, tail_tile=TAIL_TILE)
```

```python
import functools

import jax
import jax.numpy as jnp
from jax import lax
from jax.experimental import pallas as pl
from jax.experimental.pallas import tpu as pltpu

F32 = jnp.float32
BF16 = jnp.bfloat16

D_MODEL = 1024
EPS = 1e-6
LOG2_E = 1.4426950408889634
CHUNK = 64
SSD_D_INNER = 1024
SSD_HEAD_DIM = 64
SSD_HEADS = 16
SSD_GROUPS = 2
SSD_HEADS_PER_GROUP = SSD_HEADS // SSD_GROUPS
SSD_STATE = 128
SSD_CONV = 4
SSD_CONV_CH = SSD_D_INNER + 2 * SSD_GROUPS * SSD_STATE
GLA_HEADS = 4
GLA_DK = 512
GLA_DV = 1024
GLA_HEAD_K = GLA_DK // GLA_HEADS
GLA_HEAD_V = GLA_DV // GLA_HEADS
GLA_GATE_RANK = 16
GLA_TAU = 16.0
XATTN_HEADS = 4
XATTN_HEAD_DIM = D_MODEL // XATTN_HEADS
D_FF = 2816
IN_SIZES = (SSD_D_INNER, SSD_CONV_CH, SSD_HEADS, GLA_DK, GLA_DK, GLA_DV, GLA_DV, GLA_GATE_RANK,
            D_MODEL, D_MODEL)

LANES = 128
SUBLANES = 8
VMEM_LIMIT_BYTES = 56 * 1024 * 1024
MIXER_TILE = 256
GLA_FIRST_SSD_HEAD = 2
TAIL_TILE = 1024
TAIL_ROW_GROUPS = 4
PREP_STEPS = 16
PREP_SLAB = 512

OFF_Z = 0
OFF_XBC = OFF_Z + SSD_D_INNER
OFF_Q = OFF_XBC + SSD_CONV_CH
OFF_K = OFF_Q + GLA_DK
OFF_V = OFF_K + GLA_DK
OFF_R = OFF_V + GLA_DV
OFF_GS = OFF_R + GLA_DV
OFF_GG = OFF_GS + D_MODEL
OFF_SMALL = OFF_GG + D_MODEL
IN_WIDTH_PADDED = OFF_SMALL + PREP_SLAB
IN_WIDTH_ALLOC = IN_WIDTH_PADDED + PREP_SLAB
SMALL_DT = 0
SMALL_A1 = SSD_HEADS

CONV_HALO = SUBLANES
CONV_SLABS = SSD_CONV_CH // LANES
CONV_PHASES = 4
assert SSD_STATE == LANES and LANES % SSD_HEAD_DIM == 0


def _pitch(width):
    return width + LANES if (width // LANES) % SUBLANES == 0 else width


def _dot(a, b):
    return jnp.dot(a, b, preferred_element_type=F32)


def _dot_nt(a, b):
    return lax.dot_general(a, b, (((1,), (1,)), ((), ())), preferred_element_type=F32)


def _dot_tn(a, b):
    return lax.dot_general(a, b, (((0,), (0,)), ((), ())), preferred_element_type=F32)


def _rms(x, g):
    return x * lax.rsqrt(jnp.mean(x * x, axis=-1, keepdims=True) + EPS) * g


def _silu(x):
    return x * jax.nn.sigmoid(x)


def _split3(x):
    hi = x.astype(BF16)
    r1 = x - hi.astype(F32)
    mid = r1.astype(BF16)
    lo = (r1 - mid.astype(F32)).astype(BF16)
    return hi, mid, lo


def _cumsum_rows(tri, x):
    hi, mid, lo = _split3(x)
    return _dot(tri, hi) + _dot(tri, mid) + _dot(tri, lo)


def _cumsum_lanes(x, upper):
    hi, mid, lo = _split3(x)
    return _dot(hi, upper) + _dot(mid, upper) + _dot(lo, upper)


def _mixer_kernel(x_ref, gmix_ref, win_ref, convw_ref, convb_ref, dtb_ref, alog_ref, dskip_ref,
                  ssdnorm_ref, wa2_ref, ba_ref, glanorm_ref, wus_ref, wug_ref, wo_ref,
                  h_ref,
                  z_ref, xpad_ref, q_ref, k_ref, v_ref, r_ref, gs_ref, gg_ref,
                  xbc_ref, b_ref, c_ref, dt_ref, a_ref, la_ref, yt_ref, xd_ref, ys_ref,
                  qt_ref, kt_ref, kh_ref, att_ref, sb_ref, o_ref, yg_ref,
                  sstate_ref, gstate_ref, *, tile):
    s = pl.program_id(1)

    @pl.when(s == 0)
    def _():
        sstate_ref[...] = jnp.zeros_like(sstate_ref)
        gstate_ref[...] = jnp.zeros_like(gstate_ref)
        xpad_ref[:, 0:CONV_HALO, :] = jnp.zeros((CONV_SLABS, CONV_HALO, LANES), F32)

    n = _rms(x_ref[...], gmix_ref[...]).astype(BF16)

    def proj(off, width):
        return _dot(n, win_ref[:, off:off + width])

    slab = 512
    deferred = [(dst, off, c0)
                for dst, off, width in ((q_ref, OFF_Q, GLA_DK), (k_ref, OFF_K, GLA_DK),
                                        (v_ref, OFF_V, GLA_DV), (r_ref, OFF_R, GLA_DV),
                                        (z_ref, OFF_Z, SSD_D_INNER), (gs_ref, OFF_GS, D_MODEL),
                                        (gg_ref, OFF_GG, D_MODEL))
                for c0 in range(0, width, slab)]

    def emit_proj(count=1):
        for _ in range(count):
            if deferred:
                dst, off, c0 = deferred.pop(0)
                dst[:, c0:c0 + slab] = proj(off + c0, slab).astype(dst.dtype)

    small = proj(OFF_SMALL, LANES)
    for c0 in range(0, SSD_CONV_CH, slab):
        xbc = proj(OFF_XBC + c0, slab)
        for t in range(slab // LANES):
            xpad_ref[c0 // LANES + t, CONV_HALO:CONV_HALO + tile, :] = xbc[:, t * LANES:(t + 1) * LANES]

    dt = jax.nn.softplus(small + dtb_ref[...])
    dt_ref[...] = dt
    a_ref[...] = dt * (-jnp.exp(alog_ref[...]))
    logits = _dot(small.astype(BF16), wa2_ref[...]) + ba_ref[...]
    la_ref[...] = jax.nn.log_sigmoid(logits) / GLA_TAU

    rows_per_phase = tile // CONV_PHASES
    first = CONV_HALO - (SSD_CONV - 1)
    for sl in range(CONV_SLABS):
        cols = slice(sl * LANES, (sl + 1) * LANES)
        taps = {d: xpad_ref[sl, pl.ds(first + d, rows_per_phase, stride=CONV_PHASES), :]
                for d in range(CONV_PHASES + SSD_CONV - 1)}
        for phase in range(CONV_PHASES):
            acc = convb_ref[:, cols]
            for j in range(SSD_CONV):
                acc = acc + convw_ref[j:j + 1, cols] * taps[phase + j]
            xbc_ref[sl, pl.ds(phase, rows_per_phase, stride=CONV_PHASES), :] = _silu(acc)
        if sl % 2 == 1:
            emit_proj()
    xpad_ref[:, 0:CONV_HALO, :] = xpad_ref[:, tile:tile + CONV_HALO, :]
    x_slabs = SSD_D_INNER // LANES
    for g in range(SSD_GROUPS):
        b_ref[:, g * SSD_STATE:(g + 1) * SSD_STATE] = xbc_ref[x_slabs + g].astype(BF16)
        c_ref[:, g * SSD_STATE:(g + 1) * SSD_STATE] = xbc_ref[x_slabs + SSD_GROUPS + g].astype(BF16)

    rid = lax.broadcasted_iota(jnp.int32, (tile, tile), 0)
    cid = lax.broadcasted_iota(jnp.int32, (tile, tile), 1)

    def gla_stages():
        n_chunks = tile // CHUNK
        chunk_of = lambda idx: lax.shift_right_logical(idx, CHUNK.bit_length() - 1)
        tri_blocks = ((rid >= cid) & (chunk_of(rid) == chunk_of(cid))).astype(BF16)
        bcum = _cumsum_rows(tri_blocks, la_ref[...]) * LOG2_E
        blast = jnp.concatenate(
            [jnp.broadcast_to(bcum[(i + 1) * CHUNK - 1:(i + 1) * CHUNK, :], (CHUNK, GLA_DK))
             for i in range(n_chunks)], axis=0)
        kk = k_ref[...]
        yield
        qt_ref[...] = (q_ref[...] * (GLA_HEAD_K ** -0.5) * jnp.exp2(bcum)).astype(BF16)
        yield
        kt_ref[...] = (kk * jnp.exp2(-bcum)).astype(BF16)
        yield
        kh_ref[...] = (kk * jnp.exp2(blast - bcum)).astype(BF16)
        dec_t = jnp.exp2(blast).T
        causal = (lax.broadcasted_iota(jnp.int32, (CHUNK, CHUNK), 0)
                  >= lax.broadcasted_iota(jnp.int32, (CHUNK, CHUNK), 1))
        ksls = [slice(j * GLA_HEAD_K, (j + 1) * GLA_HEAD_K) for j in range(GLA_HEADS)]
        vsls = [slice(j * GLA_HEAD_V, (j + 1) * GLA_HEAD_V) for j in range(GLA_HEADS)]
        for i in range(n_chunks):
            rs = slice(i * CHUNK, (i + 1) * CHUNK)
            for j in range(GLA_HEADS):
                scores = _dot_nt(qt_ref[rs, ksls[j]], kt_ref[rs, ksls[j]])
                att_ref[i * GLA_HEADS + j] = jnp.where(causal, scores, 0.0).astype(BF16)
            yield
        states = [gstate_ref[ksl, :] for ksl in ksls]
        for i in range(n_chunks):
            rs = slice(i * CHUNK, (i + 1) * CHUNK)
            for j in range(GLA_HEADS):
                sb_ref[i, ksls[j], :] = states[j].astype(BF16)
                update = _dot_tn(kh_ref[rs, ksls[j]], v_ref[rs, vsls[j]])
                states[j] = states[j] * dec_t[ksls[j], i * CHUNK:i * CHUNK + 1] + update
            yield
        for j in range(GLA_HEADS):
            gstate_ref[ksls[j], :] = states[j]
        for i in range(n_chunks):
            rs = slice(i * CHUNK, (i + 1) * CHUNK)
            for j in range(GLA_HEADS):
                o_ref[rs, vsls[j]] = (_dot(att_ref[i * GLA_HEADS + j], v_ref[rs, vsls[j]])
                                      + _dot(qt_ref[rs, ksls[j]], sb_ref[i, ksls[j], :]))
            yield
        for j in range(GLA_HEADS):
            yg_ref[:, vsls[j]] = (_rms(o_ref[:, vsls[j]], glanorm_ref[...])
                                  * _silu(r_ref[:, vsls[j]])).astype(BF16)
            yield

    gla = gla_stages()

    upper = rid <= cid
    a_t = a_ref[...].T
    acs_t = _cumsum_lanes(a_t, upper.astype(BF16)) * LOG2_E
    acs = acs_t.T
    dt_t = dt_ref[...].T
    xs_t = [xbc_ref[sl].T for sl in range(x_slabs)]
    heads_per_slab = LANES // SSD_HEAD_DIM
    hp = SSD_HEADS_PER_GROUP * SSD_HEAD_DIM
    for g in range(SSD_GROUPS):
        nsl = slice(g * SSD_STATE, (g + 1) * SSD_STATE)
        bg = b_ref[:, nsl]
        cg = c_ref[:, nsl]
        cb_t = jnp.where(upper, _dot_nt(bg, cg), 0.0)
        state = sstate_ref[g * hp:(g + 1) * hp, :]
        y_off_t = _dot_nt(state.astype(BF16), cg)
        for r in range(SSD_HEADS_PER_GROUP):
            h = g * SSD_HEADS_PER_GROUP + r
            psl = slice(h * SSD_HEAD_DIM, (h + 1) * SSD_HEAD_DIM)
            rsl = slice(r * SSD_HEAD_DIM, (r + 1) * SSD_HEAD_DIM)
            col = acs[:, h:h + 1]
            row = acs_t[h:h + 1, :]
            last = row[:, tile - 1:tile]
            m_t = (jnp.exp2(jnp.minimum(row - col, 0.0)) * cb_t).astype(BF16)
            in_slab = (h % heads_per_slab) * SSD_HEAD_DIM
            xh_t = xs_t[h // heads_per_slab][in_slab:in_slab + SSD_HEAD_DIM, :]
            xdt_t = xh_t * dt_t[h:h + 1, :]
            y_diag_t = _dot(xdt_t.astype(BF16), m_t)
            yt_ref[psl, :] = (y_diag_t + y_off_t[rsl, :] * jnp.exp2(row)
                              + dskip_ref[h:h + 1, :] * xh_t)
            xd_ref[rsl, :] = (xdt_t * jnp.exp2(last - row)).astype(BF16)
            sstate_ref[psl, :] = state[rsl, :] * jnp.exp2(last)
            emit_proj()
            if h >= GLA_FIRST_SSD_HEAD:
                next(gla, None)
        sstate_ref[g * hp:(g + 1) * hp, :] += _dot(xd_ref[...], bg)
    emit_proj(len(deferred))
    yz = yt_ref[...].T * _silu(z_ref[...])
    gn = SSD_D_INNER // SSD_GROUPS
    for g in range(SSD_GROUPS):
        csl = slice(g * gn, (g + 1) * gn)
        ys_ref[:, csl] = _rms(yz[:, csl], ssdnorm_ref[:, csl]).astype(BF16)
    merged_ssd = jax.nn.sigmoid(gs_ref[...]) * _dot(ys_ref[...], wus_ref[:, :D_MODEL])

    for _ in gla:
        pass

    merged = merged_ssd + jax.nn.sigmoid(gg_ref[...]) * _dot(yg_ref[...], wug_ref[:, :D_MODEL])
    h_ref[...] = x_ref[...] + _dot(merged.astype(BF16), wo_ref[:, :D_MODEL])


def _memkv_kernel(mem_ref, g_ref, wkv_ref, wq_ref, wo_ref, wqk_ref, vwo_ref):
    mem_len = mem_ref.shape[0]
    m = _rms(mem_ref[...], g_ref[...]).astype(BF16)
    kv = _dot(m, wkv_ref[...].astype(BF16))
    for j in range(XATTN_HEADS):
        dsl = slice(j * XATTN_HEAD_DIM, (j + 1) * XATTN_HEAD_DIM)
        msl = slice(j * mem_len, (j + 1) * mem_len)
        k_j = kv[:, dsl].astype(BF16)
        v_j = kv[:, D_MODEL + j * XATTN_HEAD_DIM:D_MODEL + (j + 1) * XATTN_HEAD_DIM].astype(BF16)
        wq_j = wq_ref[:, dsl].astype(BF16)
        wqk_ref[:, msl] = (_dot_nt(wq_j, k_j) * (XATTN_HEAD_DIM ** -0.5)).astype(BF16)
        vwo_ref[msl, 0:D_MODEL] = _dot(v_j, wo_ref[dsl, :].astype(BF16)).astype(BF16)
    att_width = XATTN_HEADS * mem_len
    for ref, used in ((wqk_ref, att_width), (vwo_ref, D_MODEL)):
        if ref.shape[1] > used:
            ref[:, used:] = jnp.zeros((ref.shape[0], ref.shape[1] - used), BF16)


def _tail_kernel(h_ref, gx_ref, wqk_ref, vwo_ref, gf_ref, wfi_ref, wfo_ref, gfin_ref,
                 out_ref, p_ref, act_ref):
    att_width = p_ref.shape[1]
    mem_len = att_width // XATTN_HEADS
    ff_block = D_FF // 2
    rows_per_group = h_ref.shape[0] // TAIL_ROW_GROUPS
    groups = [slice(g * rows_per_group, (g + 1) * rows_per_group) for g in range(TAIL_ROW_GROUPS)]
    h1 = [h_ref[rows, :] for rows in groups]
    sc = [_dot(_rms(h, gx_ref[...]).astype(BF16), wqk_ref[:, :att_width]) for h in h1]
    for rows, s in zip(groups, sc):
        for j in range(XATTN_HEADS):
            msl = slice(j * mem_len, (j + 1) * mem_len)
            e = jnp.exp(s[:, msl] - jnp.max(s[:, msl], axis=-1, keepdims=True))
            p_ref[rows, msl] = (e / jnp.sum(e, axis=-1, keepdims=True)).astype(BF16)
    h2 = [h + _dot(p_ref[rows, :], vwo_ref[:, :D_MODEL]) for rows, h in zip(groups, h1)]
    n3 = [_rms(h, gf_ref[...]).astype(BF16) for h in h2]
    for cb in range(D_FF // ff_block):
        for rows, n in zip(groups, n3):
            gate = _dot(n, wfi_ref[:, cb * ff_block:(cb + 1) * ff_block])
            up = _dot(n, wfi_ref[:, D_FF + cb * ff_block:D_FF + (cb + 1) * ff_block])
            act_ref[rows, cb * ff_block:(cb + 1) * ff_block] = (_silu(gate) * up).astype(BF16)
    for rows, h in zip(groups, h2):
        h3 = h + _dot(act_ref[rows, :], wfo_ref[:, :D_MODEL])
        out_ref[rows, :] = _rms(h3, gfin_ref[...])


IN_BOUNDS = tuple(sum(IN_SIZES[:i]) for i in range(len(IN_SIZES) + 1))
IN_RUNS = ((OFF_Z, IN_BOUNDS[0], IN_BOUNDS[2] - IN_BOUNDS[0]),
           (OFF_Q, IN_BOUNDS[3], IN_BOUNDS[7] - IN_BOUNDS[3]),
           (OFF_GS, IN_BOUNDS[8], IN_BOUNDS[10] - IN_BOUNDS[8]))
IN_DT_COL = IN_BOUNDS[2]
IN_A1_COL = IN_BOUNDS[7]
assert all(dst % PREP_SLAB == 0 and width % PREP_SLAB == 0 for dst, _, width in IN_RUNS)
assert OFF_SMALL == (PREP_STEPS - 1) * PREP_SLAB and IN_WIDTH_PADDED == PREP_STEPS * PREP_SLAB
assert _pitch(IN_WIDTH_ALLOC) == IN_WIDTH_ALLOC


def _prep_src_row(i):
    unit = SSD_HEADS
    start = i * (PREP_SLAB // unit)
    prev_shift = 0
    for dst, src, _ in IN_RUNS:
        start = start + jnp.where(i >= dst // PREP_SLAB, (src - dst - prev_shift) // unit, 0)
        prev_shift = src - dst
    last = (IN_BOUNDS[-1] - PREP_SLAB) // unit
    return jnp.minimum(start, last) * unit


def _prep_kernel(wint_ref, dt_ref, a1_ref, *refs):
    n_plain = (len(refs) - 1) // 2
    plain_in, owin_ref, plain_out = refs[:n_plain], refs[n_plain], refs[n_plain + 1:]
    i = pl.program_id(0)

    @pl.when(i < PREP_STEPS - 1)
    def _():
        owin_ref[...] = wint_ref[...].T.astype(BF16)

    @pl.when(i == PREP_STEPS - 1)
    def _():
        pad = jnp.zeros((LANES - SSD_HEADS - GLA_GATE_RANK, D_MODEL), F32)
        small_t = jnp.concatenate([dt_ref[...], a1_ref[...], pad], axis=0)
        owin_ref[:, 0:LANES] = small_t.T.astype(BF16)
        owin_ref[:, LANES:] = jnp.zeros((D_MODEL, PREP_SLAB - LANES), BF16)

    @pl.when(i == PREP_STEPS)
    def _():
        owin_ref[...] = jnp.zeros(owin_ref.shape, BF16)

    for src_ref, dst_ref in zip(plain_in, plain_out):
        cols = src_ref.shape[1]
        dst_ref[:, 0:cols] = src_ref[...].astype(BF16)
        if dst_ref.shape[1] > cols:
            dst_ref[:, cols:] = jnp.zeros((dst_ref.shape[0], dst_ref.shape[1] - cols), BF16)


def _resident(shape):
    return pl.BlockSpec(shape, lambda b, s: (0,) * len(shape), pipeline_mode=pl.Buffered(1))


def _row(v, width=None):
    v = v.reshape(1, -1).astype(F32)
    if width is not None and v.shape[1] < width:
        v = jnp.pad(v, ((0, 0), (0, width - v.shape[1])))
    return v


@functools.partial(jax.jit, static_argnames=("tile", "tail_tile"))
def _forward(x, mem, norm_mix, w_in, ssd_conv_w, ssd_conv_b, ssd_dt_bias, ssd_A_log, ssd_D, ssd_norm,
             gla_w_a2, gla_b_a, gla_norm, w_up_ssd, w_up_gla, w_o, norm_xattn, norm_mem, w_xq, w_xkv,
             w_xo, norm_ffn, w_ffn_in, w_ffn_out, norm_final, *, tile, tail_tile):
    batch, seq, _ = x.shape
    mem_len = mem.shape[1]
    assert seq % tile == 0 and tile % CHUNK == 0
    grid = (batch, seq // tile)

    w_in_t = jnp.swapaxes(w_in[0], 0, 1)
    plain = [w_up_ssd, w_up_gla, w_o, w_ffn_in, w_ffn_out]
    row_of = lambda i: jnp.minimum(i, PREP_STEPS - 1)
    row_block_in = lambda a: pl.BlockSpec((None, a.shape[1] // PREP_STEPS, a.shape[2]),
                                          lambda i: (0, row_of(i), 0))
    row_block_out = lambda a: pl.BlockSpec((a.shape[0] // PREP_STEPS, a.shape[1]),
                                           lambda i: (row_of(i), 0))
    rows_at = lambda n, start: pl.BlockSpec((pl.Element(n), pl.Element(D_MODEL)),
                                            lambda i: (start(i), 0))
    outs = [jax.ShapeDtypeStruct((a.shape[1], _pitch(a.shape[2])), BF16) for a in plain]
    w_in_r, w_up_ssd_b, w_up_gla_b, w_o_b, w_ffn_in_b, w_ffn_out_b = pl.pallas_call(
        _prep_kernel,
        grid=(PREP_STEPS + 1,),
        in_specs=([rows_at(PREP_SLAB, _prep_src_row),
                   rows_at(SSD_HEADS, lambda i: IN_DT_COL), rows_at(GLA_GATE_RANK, lambda i: IN_A1_COL)]
                  + [row_block_in(a) for a in plain]),
        out_specs=([pl.BlockSpec((D_MODEL, PREP_SLAB), lambda i: (0, i))]
                   + [row_block_out(a) for a in outs]),
        out_shape=[jax.ShapeDtypeStruct((D_MODEL, IN_WIDTH_ALLOC), BF16)] + outs,
        compiler_params=pltpu.CompilerParams(
            dimension_semantics=("arbitrary",), vmem_limit_bytes=VMEM_LIMIT_BYTES),
        name="prep",
    )(w_in_t, w_in_t, w_in_t, *plain)
    wa2 = jnp.zeros((LANES, GLA_DK), F32).at[SMALL_A1:SMALL_A1 + GLA_GATE_RANK].set(gla_w_a2[0]).astype(BF16)
    dskip = jnp.broadcast_to(ssd_D[0].astype(F32)[:, None], (SSD_HEADS, tile))

    tok_spec = pl.BlockSpec((None, tile, D_MODEL), lambda b, s: (b, s, 0))

    mixer_inputs = [
        (x, tok_spec),
        (_row(norm_mix[0]), _resident((1, D_MODEL))),
        (w_in_r, _resident((D_MODEL, IN_WIDTH_ALLOC))),
        (ssd_conv_w[0].reshape(SSD_CONV, SSD_CONV_CH), _resident((SSD_CONV, SSD_CONV_CH))),
        (_row(ssd_conv_b[0]), _resident((1, SSD_CONV_CH))),
        (_row(ssd_dt_bias[0], LANES), _resident((1, LANES))),
        (_row(ssd_A_log[0], LANES), _resident((1, LANES))),
        (dskip, _resident((SSD_HEADS, tile))),
        (_row(ssd_norm[0]), _resident((1, SSD_D_INNER))),
        (wa2, _resident((LANES, GLA_DK))),
        (_row(gla_b_a[0]), _resident((1, GLA_DK))),
        (_row(gla_norm[0]), _resident((1, GLA_HEAD_V))),
        (w_up_ssd_b, _resident(w_up_ssd_b.shape)),
        (w_up_gla_b, _resident(w_up_gla_b.shape)),
        (w_o_b, _resident(w_o_b.shape)),
    ]
    mixer_scratch = [
        pltpu.VMEM((tile, SSD_D_INNER), F32),
        pltpu.VMEM((CONV_SLABS, tile + CONV_HALO, LANES), F32),
        pltpu.VMEM((tile, GLA_DK), F32),
        pltpu.VMEM((tile, GLA_DK), F32),
        pltpu.VMEM((tile, GLA_DV), BF16),
        pltpu.VMEM((tile, GLA_DV), F32),
        pltpu.VMEM((tile, D_MODEL), F32),
        pltpu.VMEM((tile, D_MODEL), F32),
        pltpu.VMEM((CONV_SLABS, tile, LANES), F32),
        pltpu.VMEM((tile, SSD_GROUPS * SSD_STATE), BF16),
        pltpu.VMEM((tile, SSD_GROUPS * SSD_STATE), BF16),
        pltpu.VMEM((tile, LANES), F32),
        pltpu.VMEM((tile, LANES), F32),
        pltpu.VMEM((tile, GLA_DK), F32),
        pltpu.VMEM((SSD_D_INNER, tile), F32),
        pltpu.VMEM((SSD_HEADS_PER_GROUP * SSD_HEAD_DIM, tile), BF16),
        pltpu.VMEM((tile, SSD_D_INNER), BF16),
        pltpu.VMEM((tile, GLA_DK), BF16),
        pltpu.VMEM((tile, GLA_DK), BF16),
        pltpu.VMEM((tile, GLA_DK), BF16),
        pltpu.VMEM((tile // CHUNK * GLA_HEADS, CHUNK, CHUNK), BF16),
        pltpu.VMEM((tile // CHUNK, GLA_DK, GLA_HEAD_V), BF16),
        pltpu.VMEM((tile, GLA_DV), F32),
        pltpu.VMEM((tile, GLA_DV), BF16),
        pltpu.VMEM((SSD_D_INNER, SSD_STATE), F32),
        pltpu.VMEM((GLA_DK, GLA_HEAD_V), F32),
    ]
    h1 = pl.pallas_call(
        functools.partial(_mixer_kernel, tile=tile),
        grid=grid,
        in_specs=[spec for _, spec in mixer_inputs],
        out_specs=tok_spec,
        out_shape=jax.ShapeDtypeStruct(x.shape, F32),
        scratch_shapes=mixer_scratch,
        compiler_params=pltpu.CompilerParams(
            dimension_semantics=("arbitrary", "arbitrary"), vmem_limit_bytes=VMEM_LIMIT_BYTES),
        name="mixer",
    )(*[a for a, _ in mixer_inputs])

    att_width = XATTN_HEADS * mem_len
    wqk_shape = (D_MODEL, _pitch(att_width))
    vwo_shape = (att_width, _pitch(D_MODEL))
    mem_spec = pl.BlockSpec((None, mem_len, D_MODEL), lambda b: (b, 0, 0))
    whole = lambda shape: pl.BlockSpec(shape, lambda b: (0,) * len(shape))
    layer0 = lambda a: pl.BlockSpec((None,) + a.shape[1:], lambda b: (0, 0, 0),
                                    pipeline_mode=pl.Buffered(1))
    wqk, vwo = pl.pallas_call(
        _memkv_kernel,
        grid=(batch,),
        in_specs=[mem_spec, whole((1, D_MODEL)), layer0(w_xkv), layer0(w_xq), layer0(w_xo)],
        out_specs=[pl.BlockSpec((None,) + wqk_shape, lambda b: (b, 0, 0)),
                   pl.BlockSpec((None,) + vwo_shape, lambda b: (b, 0, 0))],
        out_shape=[jax.ShapeDtypeStruct((batch,) + wqk_shape, BF16),
                   jax.ShapeDtypeStruct((batch,) + vwo_shape, BF16)],
        compiler_params=pltpu.CompilerParams(
            dimension_semantics=("arbitrary",), vmem_limit_bytes=VMEM_LIMIT_BYTES),
        name="memkv",
    )(mem, _row(norm_mem[0]), w_xkv, w_xq, w_xo)

    assert seq % tail_tile == 0
    tail_spec = pl.BlockSpec((None, tail_tile, D_MODEL), lambda b, s: (b, s, 0))
    out = pl.pallas_call(
        _tail_kernel,
        grid=(batch, seq // tail_tile),
        in_specs=[tail_spec,
                  _resident((1, D_MODEL)),
                  pl.BlockSpec((None,) + wqk_shape, lambda b, s: (b, 0, 0)),
                  pl.BlockSpec((None,) + vwo_shape, lambda b, s: (b, 0, 0)),
                  _resident((1, D_MODEL)),
                  _resident(w_ffn_in_b.shape),
                  _resident(w_ffn_out_b.shape),
                  _resident((1, D_MODEL))],
        out_specs=tail_spec,
        out_shape=jax.ShapeDtypeStruct(x.shape, F32),
        scratch_shapes=[pltpu.VMEM((tail_tile, att_width), BF16),
                        pltpu.VMEM((tail_tile, D_FF), BF16)],
        compiler_params=pltpu.CompilerParams(
            dimension_semantics=("arbitrary", "arbitrary"), vmem_limit_bytes=VMEM_LIMIT_BYTES),
        name="tail",
    )(h1, _row(norm_xattn[0]), wqk, vwo, _row(norm_ffn[0]), w_ffn_in_b, w_ffn_out_b, _row(norm_final))
    return out


def kernel(x, mem, norm_mix, w_in, ssd_conv_w, ssd_conv_b, ssd_dt_bias, ssd_A_log, ssd_D, ssd_norm,
           gla_w_a2, gla_b_a, gla_norm, w_up_ssd, w_up_gla, w_o, norm_xattn, norm_mem, w_xq, w_xkv,
           w_xo, norm_ffn, w_ffn_in, w_ffn_out, norm_final):
    return _forward(x, mem, norm_mix, w_in, ssd_conv_w, ssd_conv_b, ssd_dt_bias, ssd_A_log, ssd_D,
                    ssd_norm, gla_w_a2, gla_b_a, gla_norm, w_up_ssd, w_up_gla, w_o, norm_xattn,
                    norm_mem, w_xq, w_xkv, w_xo, norm_ffn, w_ffn_in, w_ffn_out, norm_final,
                    tile=MIXER_TILE, tail_tile=TAIL_TILE)
```

```python
import functools

import jax
import jax.numpy as jnp
from jax import lax
from jax.experimental import pallas as pl
from jax.experimental.pallas import tpu as pltpu

F32 = jnp.float32
BF16 = jnp.bfloat16

D_MODEL = 1024
EPS = 1e-6
LOG2_E = 1.4426950408889634
CHUNK = 64
SSD_D_INNER = 1024
SSD_HEAD_DIM = 64
SSD_HEADS = 16
SSD_GROUPS = 2
SSD_HEADS_PER_GROUP = SSD_HEADS // SSD_GROUPS
SSD_STATE = 128
SSD_CONV = 4
SSD_CONV_CH = SSD_D_INNER + 2 * SSD_GROUPS * SSD_STATE
GLA_HEADS = 4
GLA_DK = 512
GLA_DV = 1024
GLA_HEAD_K = GLA_DK // GLA_HEADS
GLA_HEAD_V = GLA_DV // GLA_HEADS
GLA_GATE_RANK = 16
GLA_TAU = 16.0
XATTN_HEADS = 4
XATTN_HEAD_DIM = D_MODEL // XATTN_HEADS
D_FF = 2816
IN_SIZES = (SSD_D_INNER, SSD_CONV_CH, SSD_HEADS, GLA_DK, GLA_DK, GLA_DV, GLA_DV, GLA_GATE_RANK,
            D_MODEL, D_MODEL)

LANES = 128
SUBLANES = 8
VMEM_LIMIT_BYTES = 56 * 1024 * 1024
MIXER_TILE = 256
GLA_FIRST_SSD_HEAD = 2
TAIL_TILE = 1024
TAIL_ROW_GROUPS = 4
FF_BLOCK = 256
assert D_FF % FF_BLOCK == 0
PREP_STEPS = 16
PREP_SLAB = 512

OFF_Z = 0
OFF_XBC = OFF_Z + SSD_D_INNER
OFF_Q = OFF_XBC + SSD_CONV_CH
OFF_K = OFF_Q + GLA_DK
OFF_V = OFF_K + GLA_DK
OFF_R = OFF_V + GLA_DV
OFF_GS = OFF_R + GLA_DV
OFF_GG = OFF_GS + D_MODEL
OFF_SMALL = OFF_GG + D_MODEL
IN_WIDTH_PADDED = OFF_SMALL + PREP_SLAB
IN_WIDTH_ALLOC = IN_WIDTH_PADDED + PREP_SLAB
SMALL_DT = 0
SMALL_A1 = SSD_HEADS

CONV_HALO = SUBLANES
CONV_SLABS = SSD_CONV_CH // LANES
CONV_PHASES = 4
assert SSD_STATE == LANES and LANES % SSD_HEAD_DIM == 0


def _pitch(width):
    return width + LANES if (width // LANES) % SUBLANES == 0 else width


def _dot(a, b):
    return jnp.dot(a, b, preferred_element_type=F32)


def _dot_nt(a, b):
    return lax.dot_general(a, b, (((1,), (1,)), ((), ())), preferred_element_type=F32)


def _dot_tn(a, b):
    return lax.dot_general(a, b, (((0,), (0,)), ((), ())), preferred_element_type=F32)


def _rms(x, g):
    return x * lax.rsqrt(jnp.mean(x * x, axis=-1, keepdims=True) + EPS) * g


def _silu(x):
    return x * jax.nn.sigmoid(x)


def _split3(x):
    hi = x.astype(BF16)
    r1 = x - hi.astype(F32)
    mid = r1.astype(BF16)
    lo = (r1 - mid.astype(F32)).astype(BF16)
    return hi, mid, lo


def _cumsum_rows(tri, x):
    hi, mid, lo = _split3(x)
    return _dot(tri, hi) + _dot(tri, mid) + _dot(tri, lo)


def _cumsum_lanes(x, upper):
    hi, mid, lo = _split3(x)
    return _dot(hi, upper) + _dot(mid, upper) + _dot(lo, upper)


def _mixer_kernel(x_ref, gmix_ref, win_ref, convw_ref, convb_ref, dtb_ref, alog_ref, dskip_ref,
                  ssdnorm_ref, wa2_ref, ba_ref, glanorm_ref, wus_ref, wug_ref, wo_ref,
                  h_ref,
                  z_ref, xpad_ref, q_ref, k_ref, v_ref, r_ref, gs_ref, gg_ref,
                  xbc_ref, b_ref, c_ref, dt_ref, a_ref, la_ref, yt_ref, xd_ref, ys_ref,
                  qt_ref, kt_ref, kh_ref, att_ref, sb_ref, o_ref, yg_ref,
                  sstate_ref, gstate_ref, *, tile):
    s = pl.program_id(1)

    @pl.when(s == 0)
    def _():
        sstate_ref[...] = jnp.zeros_like(sstate_ref)
        gstate_ref[...] = jnp.zeros_like(gstate_ref)
        xpad_ref[:, 0:CONV_HALO, :] = jnp.zeros((CONV_SLABS, CONV_HALO, LANES), F32)

    n = _rms(x_ref[...], gmix_ref[...]).astype(BF16)

    def proj(off, width):
        return _dot(n, win_ref[:, off:off + width])

    slab = 512
    deferred = [(dst, off, c0)
                for dst, off, width in ((q_ref, OFF_Q, GLA_DK), (k_ref, OFF_K, GLA_DK),
                                        (v_ref, OFF_V, GLA_DV), (r_ref, OFF_R, GLA_DV),
                                        (z_ref, OFF_Z, SSD_D_INNER), (gs_ref, OFF_GS, D_MODEL),
                                        (gg_ref, OFF_GG, D_MODEL))
                for c0 in range(0, width, slab)]

    def emit_proj(count=1):
        for _ in range(count):
            if deferred:
                dst, off, c0 = deferred.pop(0)
                dst[:, c0:c0 + slab] = proj(off + c0, slab).astype(dst.dtype)

    small = proj(OFF_SMALL, LANES)
    for c0 in range(0, SSD_CONV_CH, slab):
        xbc = proj(OFF_XBC + c0, slab)
        for t in range(slab // LANES):
            xpad_ref[c0 // LANES + t, CONV_HALO:CONV_HALO + tile, :] = xbc[:, t * LANES:(t + 1) * LANES]

    dt = jax.nn.softplus(small + dtb_ref[...])
    dt_ref[...] = dt
    a_ref[...] = dt * (-jnp.exp(alog_ref[...]))
    logits = _dot(small.astype(BF16), wa2_ref[...]) + ba_ref[...]
    la_ref[...] = jax.nn.log_sigmoid(logits) / GLA_TAU

    rows_per_phase = tile // CONV_PHASES
    first = CONV_HALO - (SSD_CONV - 1)
    for sl in range(CONV_SLABS):
        cols = slice(sl * LANES, (sl + 1) * LANES)
        taps = {d: xpad_ref[sl, pl.ds(first + d, rows_per_phase, stride=CONV_PHASES), :]
                for d in range(CONV_PHASES + SSD_CONV - 1)}
        for phase in range(CONV_PHASES):
            acc = convb_ref[:, cols]
            for j in range(SSD_CONV):
                acc = acc + convw_ref[j:j + 1, cols] * taps[phase + j]
            xbc_ref[sl, pl.ds(phase, rows_per_phase, stride=CONV_PHASES), :] = _silu(acc)
        if sl % 2 == 1:
            emit_proj()
    xpad_ref[:, 0:CONV_HALO, :] = xpad_ref[:, tile:tile + CONV_HALO, :]
    x_slabs = SSD_D_INNER // LANES
    for g in range(SSD_GROUPS):
        b_ref[:, g * SSD_STATE:(g + 1) * SSD_STATE] = xbc_ref[x_slabs + g].astype(BF16)
        c_ref[:, g * SSD_STATE:(g + 1) * SSD_STATE] = xbc_ref[x_slabs + SSD_GROUPS + g].astype(BF16)

    rid = lax.broadcasted_iota(jnp.int32, (tile, tile), 0)
    cid = lax.broadcasted_iota(jnp.int32, (tile, tile), 1)

    def gla_stages():
        n_chunks = tile // CHUNK
        chunk_of = lambda idx: lax.shift_right_logical(idx, CHUNK.bit_length() - 1)
        tri_blocks = ((rid >= cid) & (chunk_of(rid) == chunk_of(cid))).astype(BF16)
        bcum = _cumsum_rows(tri_blocks, la_ref[...]) * LOG2_E
        blast = jnp.concatenate(
            [jnp.broadcast_to(bcum[(i + 1) * CHUNK - 1:(i + 1) * CHUNK, :], (CHUNK, GLA_DK))
             for i in range(n_chunks)], axis=0)
        kk = k_ref[...]
        yield
        qt_ref[...] = (q_ref[...] * (GLA_HEAD_K ** -0.5) * jnp.exp2(bcum)).astype(BF16)
        yield
        kt_ref[...] = (kk * jnp.exp2(-bcum)).astype(BF16)
        yield
        kh_ref[...] = (kk * jnp.exp2(blast - bcum)).astype(BF16)
        dec_t = jnp.exp2(blast).T
        causal = (lax.broadcasted_iota(jnp.int32, (CHUNK, CHUNK), 0)
                  >= lax.broadcasted_iota(jnp.int32, (CHUNK, CHUNK), 1))
        ksls = [slice(j * GLA_HEAD_K, (j + 1) * GLA_HEAD_K) for j in range(GLA_HEADS)]
        vsls = [slice(j * GLA_HEAD_V, (j + 1) * GLA_HEAD_V) for j in range(GLA_HEADS)]
        for i in range(n_chunks):
            rs = slice(i * CHUNK, (i + 1) * CHUNK)
            for j in range(GLA_HEADS):
                scores = _dot_nt(qt_ref[rs, ksls[j]], kt_ref[rs, ksls[j]])
                att_ref[i * GLA_HEADS + j] = jnp.where(causal, scores, 0.0).astype(BF16)
            yield
        states = [gstate_ref[ksl, :] for ksl in ksls]
        for i in range(n_chunks):
            rs = slice(i * CHUNK, (i + 1) * CHUNK)
            for j in range(GLA_HEADS):
                sb_ref[i, ksls[j], :] = states[j].astype(BF16)
                update = _dot_tn(kh_ref[rs, ksls[j]], v_ref[rs, vsls[j]])
                states[j] = states[j] * dec_t[ksls[j], i * CHUNK:i * CHUNK + 1] + update
            yield
        for j in range(GLA_HEADS):
            gstate_ref[ksls[j], :] = states[j]
        for i in range(n_chunks):
            rs = slice(i * CHUNK, (i + 1) * CHUNK)
            for j in range(GLA_HEADS):
                o_ref[rs, vsls[j]] = (_dot(att_ref[i * GLA_HEADS + j], v_ref[rs, vsls[j]])
                                      + _dot(qt_ref[rs, ksls[j]], sb_ref[i, ksls[j], :]))
            yield
        for j in range(GLA_HEADS):
            yg_ref[:, vsls[j]] = (_rms(o_ref[:, vsls[j]], glanorm_ref[...])
                                  * _silu(r_ref[:, vsls[j]])).astype(BF16)
            yield

    gla = gla_stages()

    upper = rid <= cid
    a_t = a_ref[...].T
    acs_t = _cumsum_lanes(a_t, upper.astype(BF16)) * LOG2_E
    acs = acs_t.T
    dt_t = dt_ref[...].T
    xs_t = [xbc_ref[sl].T for sl in range(x_slabs)]
    heads_per_slab = LANES // SSD_HEAD_DIM
    hp = SSD_HEADS_PER_GROUP * SSD_HEAD_DIM
    for g in range(SSD_GROUPS):
        nsl = slice(g * SSD_STATE, (g + 1) * SSD_STATE)
        bg = b_ref[:, nsl]
        cg = c_ref[:, nsl]
        cb_t = jnp.where(upper, _dot_nt(bg, cg), 0.0)
        state = sstate_ref[g * hp:(g + 1) * hp, :]
        y_off_t = _dot_nt(state.astype(BF16), cg)
        for r in range(SSD_HEADS_PER_GROUP):
            h = g * SSD_HEADS_PER_GROUP + r
            psl = slice(h * SSD_HEAD_DIM, (h + 1) * SSD_HEAD_DIM)
            rsl = slice(r * SSD_HEAD_DIM, (r + 1) * SSD_HEAD_DIM)
            col = acs[:, h:h + 1]
            row = acs_t[h:h + 1, :]
            last = row[:, tile - 1:tile]
            m_t = (jnp.exp2(jnp.minimum(row - col, 0.0)) * cb_t).astype(BF16)
            in_slab = (h % heads_per_slab) * SSD_HEAD_DIM
            xh_t = xs_t[h // heads_per_slab][in_slab:in_slab + SSD_HEAD_DIM, :]
            xdt_t = xh_t * dt_t[h:h + 1, :]
            y_diag_t = _dot(xdt_t.astype(BF16), m_t)
            yt_ref[psl, :] = (y_diag_t + y_off_t[rsl, :] * jnp.exp2(row)
                              + dskip_ref[h:h + 1, :] * xh_t)
            xd_ref[rsl, :] = (xdt_t * jnp.exp2(last - row)).astype(BF16)
            sstate_ref[psl, :] = state[rsl, :] * jnp.exp2(last)
            emit_proj()
            if h >= GLA_FIRST_SSD_HEAD:
                next(gla, None)
        sstate_ref[g * hp:(g + 1) * hp, :] += _dot(xd_ref[...], bg)
    emit_proj(len(deferred))
    yz = yt_ref[...].T * _silu(z_ref[...])
    gn = SSD_D_INNER // SSD_GROUPS
    for g in range(SSD_GROUPS):
        csl = slice(g * gn, (g + 1) * gn)
        ys_ref[:, csl] = _rms(yz[:, csl], ssdnorm_ref[:, csl]).astype(BF16)
    merged_ssd = jax.nn.sigmoid(gs_ref[...]) * _dot(ys_ref[...], wus_ref[:, :D_MODEL])

    for _ in gla:
        pass

    merged = merged_ssd + jax.nn.sigmoid(gg_ref[...]) * _dot(yg_ref[...], wug_ref[:, :D_MODEL])
    h_ref[...] = x_ref[...] + _dot(merged.astype(BF16), wo_ref[:, :D_MODEL])


def _memkv_kernel(mem_ref, g_ref, wkv_ref, wq_ref, wo_ref, wqk_ref, vwo_ref):
    mem_len = mem_ref.shape[0]
    m = _rms(mem_ref[...], g_ref[...]).astype(BF16)
    kv = _dot(m, wkv_ref[...].astype(BF16))
    for j in range(XATTN_HEADS):
        dsl = slice(j * XATTN_HEAD_DIM, (j + 1) * XATTN_HEAD_DIM)
        msl = slice(j * mem_len, (j + 1) * mem_len)
        k_j = kv[:, dsl].astype(BF16)
        v_j = kv[:, D_MODEL + j * XATTN_HEAD_DIM:D_MODEL + (j + 1) * XATTN_HEAD_DIM].astype(BF16)
        wq_j = wq_ref[:, dsl].astype(BF16)
        wqk_ref[:, msl] = (_dot_nt(wq_j, k_j) * (XATTN_HEAD_DIM ** -0.5)).astype(BF16)
        vwo_ref[msl, 0:D_MODEL] = _dot(v_j, wo_ref[dsl, :].astype(BF16)).astype(BF16)
    att_width = XATTN_HEADS * mem_len
    for ref, used in ((wqk_ref, att_width), (vwo_ref, D_MODEL)):
        if ref.shape[1] > used:
            ref[:, used:] = jnp.zeros((ref.shape[0], ref.shape[1] - used), BF16)


def _tail_kernel(h_ref, gx_ref, wqk_ref, vwo_ref, gf_ref, wfi_ref, wfo_ref, gfin_ref,
                 out_ref, p_ref, act_ref):
    att_width = p_ref.shape[1]
    mem_len = att_width // XATTN_HEADS
    ff_block = FF_BLOCK
    rows_per_group = h_ref.shape[0] // TAIL_ROW_GROUPS
    groups = [slice(g * rows_per_group, (g + 1) * rows_per_group) for g in range(TAIL_ROW_GROUPS)]
    h1 = [h_ref[rows, :] for rows in groups]
    sc = [_dot(_rms(h, gx_ref[...]).astype(BF16), wqk_ref[:, :att_width]) for h in h1]
    for rows, s in zip(groups, sc):
        for j in range(XATTN_HEADS):
            msl = slice(j * mem_len, (j + 1) * mem_len)
            e = jnp.exp(s[:, msl] - jnp.max(s[:, msl], axis=-1, keepdims=True))
            p_ref[rows, msl] = (e / jnp.sum(e, axis=-1, keepdims=True)).astype(BF16)
    h2 = [h + _dot(p_ref[rows, :], vwo_ref[:, :D_MODEL]) for rows, h in zip(groups, h1)]
    n3 = [_rms(h, gf_ref[...]).astype(BF16) for h in h2]
    for cb in range(D_FF // ff_block):
        for rows, n in zip(groups, n3):
            gate = _dot(n, wfi_ref[:, cb * ff_block:(cb + 1) * ff_block])
            up = _dot(n, wfi_ref[:, D_FF + cb * ff_block:D_FF + (cb + 1) * ff_block])
            act_ref[rows, cb * ff_block:(cb + 1) * ff_block] = (_silu(gate) * up).astype(BF16)
    for rows, h in zip(groups, h2):
        h3 = h + _dot(act_ref[rows, :], wfo_ref[:, :D_MODEL])
        out_ref[rows, :] = _rms(h3, gfin_ref[...])


IN_BOUNDS = tuple(sum(IN_SIZES[:i]) for i in range(len(IN_SIZES) + 1))
IN_RUNS = ((OFF_Z, IN_BOUNDS[0], IN_BOUNDS[2] - IN_BOUNDS[0]),
           (OFF_Q, IN_BOUNDS[3], IN_BOUNDS[7] - IN_BOUNDS[3]),
           (OFF_GS, IN_BOUNDS[8], IN_BOUNDS[10] - IN_BOUNDS[8]))
IN_DT_COL = IN_BOUNDS[2]
IN_A1_COL = IN_BOUNDS[7]
assert all(dst % PREP_SLAB == 0 and width % PREP_SLAB == 0 for dst, _, width in IN_RUNS)
assert OFF_SMALL == (PREP_STEPS - 1) * PREP_SLAB and IN_WIDTH_PADDED == PREP_STEPS * PREP_SLAB
assert _pitch(IN_WIDTH_ALLOC) == IN_WIDTH_ALLOC


def _prep_src_row(i):
    unit = SSD_HEADS
    start = i * (PREP_SLAB // unit)
    prev_shift = 0
    for dst, src, _ in IN_RUNS:
        start = start + jnp.where(i >= dst // PREP_SLAB, (src - dst - prev_shift) // unit, 0)
        prev_shift = src - dst
    last = (IN_BOUNDS[-1] - PREP_SLAB) // unit
    return jnp.minimum(start, last) * unit


def _prep_kernel(wint_ref, dt_ref, a1_ref, *refs):
    n_plain = (len(refs) - 1) // 2
    plain_in, owin_ref, plain_out = refs[:n_plain], refs[n_plain], refs[n_plain + 1:]
    i = pl.program_id(0)

    @pl.when(i < PREP_STEPS - 1)
    def _():
        owin_ref[...] = wint_ref[...].T.astype(BF16)

    @pl.when(i == PREP_STEPS - 1)
    def _():
        pad = jnp.zeros((LANES - SSD_HEADS - GLA_GATE_RANK, D_MODEL), F32)
        small_t = jnp.concatenate([dt_ref[...], a1_ref[...], pad], axis=0)
        owin_ref[:, 0:LANES] = small_t.T.astype(BF16)
        owin_ref[:, LANES:] = jnp.zeros((D_MODEL, PREP_SLAB - LANES), BF16)

    @pl.when(i == PREP_STEPS)
    def _():
        owin_ref[...] = jnp.zeros(owin_ref.shape, BF16)

    for src_ref, dst_ref in zip(plain_in, plain_out):
        cols = src_ref.shape[1]
        dst_ref[:, 0:cols] = src_ref[...].astype(BF16)
        if dst_ref.shape[1] > cols:
            dst_ref[:, cols:] = jnp.zeros((dst_ref.shape[0], dst_ref.shape[1] - cols), BF16)


def _resident(shape):
    return pl.BlockSpec(shape, lambda b, s: (0,) * len(shape), pipeline_mode=pl.Buffered(1))


def _row(v, width=None):
    v = v.reshape(1, -1).astype(F32)
    if width is not None and v.shape[1] < width:
        v = jnp.pad(v, ((0, 0), (0, width - v.shape[1])))
    return v


@functools.partial(jax.jit, static_argnames=("tile", "tail_tile"))
def _forward(x, mem, norm_mix, w_in, ssd_conv_w, ssd_conv_b, ssd_dt_bias, ssd_A_log, ssd_D, ssd_norm,
             gla_w_a2, gla_b_a, gla_norm, w_up_ssd, w_up_gla, w_o, norm_xattn, norm_mem, w_xq, w_xkv,
             w_xo, norm_ffn, w_ffn_in, w_ffn_out, norm_final, *, tile, tail_tile):
    batch, seq, _ = x.shape
    mem_len = mem.shape[1]
    assert seq % tile == 0 and tile % CHUNK == 0
    grid = (batch, seq // tile)

    w_in_t = jnp.swapaxes(w_in[0], 0, 1)
    plain = [w_up_ssd, w_up_gla, w_o, w_ffn_in, w_ffn_out]
    row_of = lambda i: jnp.minimum(i, PREP_STEPS - 1)
    row_block_in = lambda a: pl.BlockSpec((None, a.shape[1] // PREP_STEPS, a.shape[2]),
                                          lambda i: (0, row_of(i), 0))
    row_block_out = lambda a: pl.BlockSpec((a.shape[0] // PREP_STEPS, a.shape[1]),
                                           lambda i: (row_of(i), 0))
    rows_at = lambda n, start: pl.BlockSpec((pl.Element(n), pl.Element(D_MODEL)),
                                            lambda i: (start(i), 0))
    outs = [jax.ShapeDtypeStruct((a.shape[1], _pitch(a.shape[2])), BF16) for a in plain]
    w_in_r, w_up_ssd_b, w_up_gla_b, w_o_b, w_ffn_in_b, w_ffn_out_b = pl.pallas_call(
        _prep_kernel,
        grid=(PREP_STEPS + 1,),
        in_specs=([rows_at(PREP_SLAB, _prep_src_row),
                   rows_at(SSD_HEADS, lambda i: IN_DT_COL), rows_at(GLA_GATE_RANK, lambda i: IN_A1_COL)]
                  + [row_block_in(a) for a in plain]),
        out_specs=([pl.BlockSpec((D_MODEL, PREP_SLAB), lambda i: (0, i))]
                   + [row_block_out(a) for a in outs]),
        out_shape=[jax.ShapeDtypeStruct((D_MODEL, IN_WIDTH_ALLOC), BF16)] + outs,
        compiler_params=pltpu.CompilerParams(
            dimension_semantics=("arbitrary",), vmem_limit_bytes=VMEM_LIMIT_BYTES),
        name="prep",
    )(w_in_t, w_in_t, w_in_t, *plain)
    wa2 = jnp.zeros((LANES, GLA_DK), F32).at[SMALL_A1:SMALL_A1 + GLA_GATE_RANK].set(gla_w_a2[0]).astype(BF16)
    dskip = jnp.broadcast_to(ssd_D[0].astype(F32)[:, None], (SSD_HEADS, tile))

    tok_spec = pl.BlockSpec((None, tile, D_MODEL), lambda b, s: (b, s, 0))

    mixer_inputs = [
        (x, tok_spec),
        (_row(norm_mix[0]), _resident((1, D_MODEL))),
        (w_in_r, _resident((D_MODEL, IN_WIDTH_ALLOC))),
        (ssd_conv_w[0].reshape(SSD_CONV, SSD_CONV_CH), _resident((SSD_CONV, SSD_CONV_CH))),
        (_row(ssd_conv_b[0]), _resident((1, SSD_CONV_CH))),
        (_row(ssd_dt_bias[0], LANES), _resident((1, LANES))),
        (_row(ssd_A_log[0], LANES), _resident((1, LANES))),
        (dskip, _resident((SSD_HEADS, tile))),
        (_row(ssd_norm[0]), _resident((1, SSD_D_INNER))),
        (wa2, _resident((LANES, GLA_DK))),
        (_row(gla_b_a[0]), _resident((1, GLA_DK))),
        (_row(gla_norm[0]), _resident((1, GLA_HEAD_V))),
        (w_up_ssd_b, _resident(w_up_ssd_b.shape)),
        (w_up_gla_b, _resident(w_up_gla_b.shape)),
        (w_o_b, _resident(w_o_b.shape)),
    ]
    mixer_scratch = [
        pltpu.VMEM((tile, SSD_D_INNER), F32),
        pltpu.VMEM((CONV_SLABS, tile + CONV_HALO, LANES), F32),
        pltpu.VMEM((tile, GLA_DK), F32),
        pltpu.VMEM((tile, GLA_DK), F32),
        pltpu.VMEM((tile, GLA_DV), BF16),
        pltpu.VMEM((tile, GLA_DV), F32),
        pltpu.VMEM((tile, D_MODEL), F32),
        pltpu.VMEM((tile, D_MODEL), F32),
        pltpu.VMEM((CONV_SLABS, tile, LANES), F32),
        pltpu.VMEM((tile, SSD_GROUPS * SSD_STATE), BF16),
        pltpu.VMEM((tile, SSD_GROUPS * SSD_STATE), BF16),
        pltpu.VMEM((tile, LANES), F32),
        pltpu.VMEM((tile, LANES), F32),
        pltpu.VMEM((tile, GLA_DK), F32),
        pltpu.VMEM((SSD_D_INNER, tile), F32),
        pltpu.VMEM((SSD_HEADS_PER_GROUP * SSD_HEAD_DIM, tile), BF16),
        pltpu.VMEM((tile, SSD_D_INNER), BF16),
        pltpu.VMEM((tile, GLA_DK), BF16),
        pltpu.VMEM((tile, GLA_DK), BF16),
        pltpu.VMEM((tile, GLA_DK), BF16),
        pltpu.VMEM((tile // CHUNK * GLA_HEADS, CHUNK, CHUNK), BF16),
        pltpu.VMEM((tile // CHUNK, GLA_DK, GLA_HEAD_V), BF16),
        pltpu.VMEM((tile, GLA_DV), F32),
        pltpu.VMEM((tile, GLA_DV), BF16),
        pltpu.VMEM((SSD_D_INNER, SSD_STATE), F32),
        pltpu.VMEM((GLA_DK, GLA_HEAD_V), F32),
    ]
    h1 = pl.pallas_call(
        functools.partial(_mixer_kernel, tile=tile),
        grid=grid,
        in_specs=[spec for _, spec in mixer_inputs],
        out_specs=tok_spec,
        out_shape=jax.ShapeDtypeStruct(x.shape, F32),
        scratch_shapes=mixer_scratch,
        compiler_params=pltpu.CompilerParams(
            dimension_semantics=("arbitrary", "arbitrary"), vmem_limit_bytes=VMEM_LIMIT_BYTES),
        name="mixer",
    )(*[a for a, _ in mixer_inputs])

    att_width = XATTN_HEADS * mem_len
    wqk_shape = (D_MODEL, _pitch(att_width))
    vwo_shape = (att_width, _pitch(D_MODEL))
    mem_spec = pl.BlockSpec((None, mem_len, D_MODEL), lambda b: (b, 0, 0))
    whole = lambda shape: pl.BlockSpec(shape, lambda b: (0,) * len(shape))
    layer0 = lambda a: pl.BlockSpec((None,) + a.shape[1:], lambda b: (0, 0, 0),
                                    pipeline_mode=pl.Buffered(1))
    wqk, vwo = pl.pallas_call(
        _memkv_kernel,
        grid=(batch,),
        in_specs=[mem_spec, whole((1, D_MODEL)), layer0(w_xkv), layer0(w_xq), layer0(w_xo)],
        out_specs=[pl.BlockSpec((None,) + wqk_shape, lambda b: (b, 0, 0)),
                   pl.BlockSpec((None,) + vwo_shape, lambda b: (b, 0, 0))],
        out_shape=[jax.ShapeDtypeStruct((batch,) + wqk_shape, BF16),
                   jax.ShapeDtypeStruct((batch,) + vwo_shape, BF16)],
        compiler_params=pltpu.CompilerParams(
            dimension_semantics=("arbitrary",), vmem_limit_bytes=VMEM_LIMIT_BYTES),
        name="memkv",
    )(mem, _row(norm_mem[0]), w_xkv, w_xq, w_xo)

    assert seq % tail_tile == 0
    tail_spec = pl.BlockSpec((None, tail_tile, D_MODEL), lambda b, s: (b, s, 0))
    out = pl.pallas_call(
        _tail_kernel,
        grid=(batch, seq // tail_tile),
        in_specs=[tail_spec,
                  _resident((1, D_MODEL)),
                  pl.BlockSpec((None,) + wqk_shape, lambda b, s: (b, 0, 0)),
                  pl.BlockSpec((None,) + vwo_shape, lambda b, s: (b, 0, 0)),
                  _resident((1, D_MODEL)),
                  _resident(w_ffn_in_b.shape),
                  _resident(w_ffn_out_b.shape),
                  _resident((1, D_MODEL))],
        out_specs=tail_spec,
        out_shape=jax.ShapeDtypeStruct(x.shape, F32),
        scratch_shapes=[pltpu.VMEM((tail_tile, att_width), BF16),
                        pltpu.VMEM((tail_tile, D_FF), BF16)],
        compiler_params=pltpu.CompilerParams(
            dimension_semantics=("arbitrary", "arbitrary"), vmem_limit_bytes=VMEM_LIMIT_BYTES),
        name="tail",
    )(h1, _row(norm_xattn[0]), wqk, vwo, _row(norm_ffn[0]), w_ffn_in_b, w_ffn_out_b, _row(norm_final))
    return out


def kernel(x, mem, norm_mix, w_in, ssd_conv_w, ssd_conv_b, ssd_dt_bias, ssd_A_log, ssd_D, ssd_norm,
           gla_w_a2, gla_b_a, gla_norm, w_up_ssd, w_up_gla, w_o, norm_xattn, norm_mem, w_xq, w_xkv,
           w_xo, norm_ffn, w_ffn_in, w_ffn_out, norm_final):
    return _forward(x, mem, norm_mix, w_in, ssd_conv_w, ssd_conv_b, ssd_dt_bias, ssd_A_log, ssd_D,
                    ssd_norm, gla_w_a2, gla_b_a, gla_norm, w_up_ssd, w_up_gla, w_o, norm_xattn,
                    norm_mem, w_xq, w_xkv, w_xo, norm_ffn, w_ffn_in, w_ffn_out, norm_final,
                    tile=MIXER_TILE, tail_tile=TAIL_TILE)
```
